```python
import jax, jax.numpy as jnp
from jax import lax
import numpy as np

D_MODEL = 1024
BATCH = 2
SEQ = 8192
DEPTH = 1
DEC_BATCH = 32
DEC_SEQ = 8
PAST_LEN = 16384
PAGE_SIZE = 128

N_HEADS_A = 8
HEAD_DIM_A = 64
WIDTH_A = N_HEADS_A * HEAD_DIM_A
Q_BLOCK = 128
FORGET_BIAS = 7.0
N_GROUPS_B = 8
WIDTH_B = 512
GROUP_DIM_B = WIDTH_B // N_GROUPS_B
CHUNK = 128
N_EXPERT_GROUPS = 4
EXPERTS_PER_GROUP = 8
N_EXPERTS = N_EXPERT_GROUPS * EXPERTS_PER_GROUP
TOP_K = 2
D_FF_EXPERT = 512
DISPATCH_BLOCK = 128
PLE_DIM = 256
EPS = 1e-6
OFF_Q = 0
OFF_K = OFF_Q + WIDTH_A
OFF_V = OFF_K + WIDTH_A
OFF_F = OFF_V + WIDTH_A
OFF_U = OFF_F + N_HEADS_A
OFF_VB = OFF_U + WIDTH_B
OFF_GA = OFF_VB + WIDTH_B
OFF_GB = OFF_GA + D_MODEL
PROJ_COLS = OFF_GB + D_MODEL

kernel_name = 'fox_gmlp_hiermoe_hybrid_step'


def rmsnorm(x, g):
    xf = x.astype(jnp.float32)
    y = xf * lax.rsqrt(jnp.mean(xf * xf, axis=-1, keepdims=True) + EPS)
    return (y * g.astype(jnp.float32)).astype(x.dtype)


def layernorm(x, g, b):
    xf = x.astype(jnp.float32)
    mu = jnp.mean(xf, axis=-1, keepdims=True)
    xc = xf - mu
    y = xc * lax.rsqrt(jnp.mean(xc * xc, axis=-1, keepdims=True) + EPS)
    return (y * g.astype(jnp.float32) + b.astype(jnp.float32)).astype(x.dtype)


def project_inputs(h, w_in, b_f):
    B, S = h.shape[0], h.shape[1]
    proj = jnp.einsum('bsd,dc->bsc', h, w_in)
    q = proj[..., OFF_Q:OFF_K].reshape(B, S, N_HEADS_A, HEAD_DIM_A)
    k = proj[..., OFF_K:OFF_V].reshape(B, S, N_HEADS_A, HEAD_DIM_A)
    v = proj[..., OFF_V:OFF_F].reshape(B, S, N_HEADS_A, HEAD_DIM_A)
    logf = jax.nn.log_sigmoid((proj[..., OFF_F:OFF_U] + b_f).astype(jnp.float32))
    u = proj[..., OFF_U:OFF_VB]
    vb = proj[..., OFF_VB:OFF_GA]
    ga = jax.nn.sigmoid(proj[..., OFF_GA:OFF_GB])
    gb = jax.nn.sigmoid(proj[..., OFF_GB:PROJ_COLS])
    return q, k, v, logf, u, vb, ga, gb


def fox_prompt(q, k, v, logf):
    B, S, H, Dh = q.shape
    nb = S // Q_BLOCK
    scale = Dh ** -0.5
    c = jnp.cumsum(logf, axis=1).transpose(0, 2, 1)
    key_pos = jnp.arange(S)
    qb = q.reshape(B, nb, Q_BLOCK, H, Dh).transpose(1, 0, 2, 3, 4)
    cb = c.reshape(B, H, nb, Q_BLOCK).transpose(2, 0, 1, 3)
    pb = key_pos.reshape(nb, Q_BLOCK)

    def block(args):
        qi, ci, pi = args
        s = jnp.einsum('bqhd,bkhd->bhqk', qi, k).astype(jnp.float32) * scale
        s = s + ci[..., :, None] - c[:, :, None, :]
        s = jnp.where(pi[:, None] >= key_pos[None, :], s, -jnp.inf)
        p = jax.nn.softmax(s, axis=-1)
        return jnp.einsum('bhqk,bkhd->bqhd', p.astype(v.dtype), v)

    o = lax.map(block, (qb, cb, pb))
    return o.transpose(1, 0, 2, 3, 4).reshape(B, S, H * Dh)


def fox_sample(q, k, v, logf, cache_k, cache_v, cache_logf, page_table):
    DB, T, H, Dh = q.shape
    past = page_table.shape[1] * cache_k.shape[1]
    scale = Dh ** -0.5
    kp = cache_k[page_table].reshape(DB, past, H, Dh)
    vp = cache_v[page_table].reshape(DB, past, H, Dh)
    lfp = cache_logf[page_table].reshape(DB, past, H).astype(jnp.float32)
    suffix = (lax.cumsum(lfp, axis=1, reverse=True) - lfp).transpose(0, 2, 1)
    cn = jnp.cumsum(logf, axis=1).transpose(0, 2, 1)
    s_past = jnp.einsum('bqhd,bkhd->bhqk', q, kp).astype(jnp.float32) * scale
    s_past = s_past + suffix[:, :, None, :] + cn[:, :, :, None]
    s_new = jnp.einsum('bqhd,bkhd->bhqk', q, k).astype(jnp.float32) * scale
    s_new = s_new + cn[:, :, :, None] - cn[:, :, None, :]
    causal = jnp.tril(jnp.ones((T, T), dtype=bool))
    s_new = jnp.where(causal, s_new, -jnp.inf)
    p = jax.nn.softmax(jnp.concatenate([s_past, s_new], axis=-1), axis=-1)
    o = (jnp.einsum('bhqk,bkhd->bqhd', p[..., :past].astype(v.dtype), vp)
         + jnp.einsum('bhqk,bkhd->bqhd', p[..., past:].astype(v.dtype), v))
    return o.reshape(DB, T, H * Dh)


def chunk_mlp(u, vb, ln_g, ln_b, w_s, b_s):
    B, S, _ = u.shape
    L = min(S, CHUNK)
    nc = S // L
    ug = jax.nn.gelu(u)
    v = layernorm(jax.nn.gelu(vb), ln_g, ln_b)
    vc = v.reshape(B, nc, L, N_GROUPS_B, GROUP_DIM_B)
    w = w_s[:, :L, :L] * jnp.tril(jnp.ones((L, L), dtype=w_s.dtype))
    sv = jnp.einsum('gts,bcsgd->bctgd', w, vc) + b_s[:, :L].T[:, :, None]
    return ug * sv.reshape(B, S, WIDTH_B), v


def merge_branches(x, a_out, b_out, ga, gb, w_up_a, w_up_b, w_o):
    m = ga * (a_out @ w_up_a) + gb * (b_out @ w_up_b)
    return x + m @ w_o


def hier_moe(h, w_rg, b_rg, w_re, b_re, w_g, w_u, w_d):
    T, D = h.shape
    hf = h.astype(jnp.float32)
    glog = hf @ w_rg.astype(jnp.float32) + b_rg.astype(jnp.float32)
    gprob = jax.nn.softmax(glog, axis=-1)
    gsel = jnp.argmax(glog, axis=-1).astype(jnp.int32)
    pg = jnp.take_along_axis(gprob, gsel[:, None], axis=1)[:, 0]
    elog = (hf @ w_re.astype(jnp.float32) + b_re.astype(jnp.float32)).reshape(T, N_EXPERT_GROUPS, EXPERTS_PER_GROUP)
    elog = jnp.take_along_axis(elog, gsel[:, None, None], axis=1)[:, 0]
    eprob = jax.nn.softmax(elog, axis=-1)
    topv, topi = lax.top_k(eprob, TOP_K)
    topv = topv / jnp.sum(topv, axis=-1, keepdims=True)
    expert_idx = gsel[:, None] * EXPERTS_PER_GROUP + topi.astype(jnp.int32)
    wts = pg[:, None] * topv

    N = T * TOP_K
    flat_e = expert_idx.reshape(N)
    flat_tok = jnp.arange(N, dtype=jnp.int32) // TOP_K
    flat_w = wts.reshape(N)
    order = jnp.argsort(flat_e)
    se = flat_e[order]
    counts = jnp.zeros((N_EXPERTS,), jnp.int32).at[flat_e].add(1)
    padded = (counts + DISPATCH_BLOCK - 1) // DISPATCH_BLOCK * DISPATCH_BLOCK
    start = jnp.cumsum(counts) - counts
    pend = jnp.cumsum(padded)
    pstart = pend - padded
    dest = pstart[se] + (jnp.arange(N, dtype=jnp.int32) - start[se])
    nb = -(-(N + N_EXPERTS * (DISPATCH_BLOCK - 1)) // DISPATCH_BLOCK)
    P = nb * DISPATCH_BLOCK
    slot_tok = jnp.full((P,), T, jnp.int32).at[dest].set(flat_tok[order])
    slot_w = jnp.zeros((P,), jnp.float32).at[dest].set(flat_w[order])
    block_start = jnp.arange(nb, dtype=jnp.int32) * DISPATCH_BLOCK
    block_e = jnp.minimum(jnp.searchsorted(pend, block_start, side='right'), N_EXPERTS - 1)
    hp = jnp.concatenate([h, jnp.zeros((1, D), h.dtype)], axis=0)
    xb = hp[slot_tok].reshape(nb, DISPATCH_BLOCK, D)

    def run_block(args):
        xi, e = args
        return (jax.nn.silu(xi @ w_g[e]) * (xi @ w_u[e])) @ w_d[e]

    yb = lax.map(run_block, (xb, block_e)).reshape(P, D)
    out = jnp.zeros((T + 1, D), jnp.float32).at[slot_tok].add(yb.astype(jnp.float32) * slot_w[:, None])
    return out[:T].astype(h.dtype)


def channel_and_ple(x, p_i, norm_moe_g, w_rg, b_rg, w_re, b_re, w_g, w_u, w_d,
                    norm_ple_g, w_ple_gate, b_ple_gate, w_ple_proj, norm_ple_post_g):
    B, S, D = x.shape
    h = rmsnorm(x, norm_moe_g).reshape(B * S, D)
    x = x + hier_moe(h, w_rg, b_rg, w_re, b_re, w_g, w_u, w_d).reshape(B, S, D)
    gate = jax.nn.sigmoid(rmsnorm(x, norm_ple_g) @ w_ple_gate + b_ple_gate)
    e = rmsnorm(p_i @ w_ple_proj, norm_ple_post_g)
    return x + gate * e


def setup_inputs(seed: int = 0) -> dict:
    key = jax.random.key(seed)
    ks = jax.random.split(key, 40)
    n_pages = PAST_LEN // PAGE_SIZE
    n_used = DEC_BATCH * n_pages
    n_pool = (n_used * 5) // 4

    def nrm(k, shape, scale=1.0):
        return jax.random.normal(k, shape, jnp.float32) * scale

    page_table = jax.random.permutation(ks[8], n_pool)[:n_used].reshape(DEC_BATCH, n_pages).astype(jnp.int32)
    return {
        'x_prompt': nrm(ks[0], (BATCH, SEQ, D_MODEL)),
        'x_sample': nrm(ks[1], (DEC_BATCH, DEC_SEQ, D_MODEL)),
        'p_prompt': nrm(ks[2], (DEPTH, BATCH, SEQ, PLE_DIM)),
        'p_sample': nrm(ks[3], (DEPTH, DEC_BATCH, DEC_SEQ, PLE_DIM)),
        'cache_k': nrm(ks[4], (DEPTH, n_pool, PAGE_SIZE, N_HEADS_A, HEAD_DIM_A)),
        'cache_v': nrm(ks[5], (DEPTH, n_pool, PAGE_SIZE, N_HEADS_A, HEAD_DIM_A)),
        'cache_logf': jax.nn.log_sigmoid(FORGET_BIAS + nrm(ks[6], (DEPTH, n_pool, PAGE_SIZE, N_HEADS_A), 0.5)),
        'page_table': page_table,
        'norm_mix_g': 1.0 + nrm(ks[9], (DEPTH, D_MODEL), 0.05),
        'w_in': nrm(ks[10], (DEPTH, D_MODEL, PROJ_COLS), D_MODEL ** -0.5),
        'b_f': FORGET_BIAS + nrm(ks[11], (DEPTH, N_HEADS_A), 0.5),
        'ln_v_g': 1.0 + nrm(ks[12], (DEPTH, WIDTH_B), 0.05),
        'ln_v_b': nrm(ks[13], (DEPTH, WIDTH_B), 0.02),
        'w_spatial': nrm(ks[14], (DEPTH, N_GROUPS_B, CHUNK, CHUNK), 0.5 * CHUNK ** -0.5),
        'b_spatial': 1.0 + nrm(ks[15], (DEPTH, N_GROUPS_B, CHUNK), 0.05),
        'w_up_a': nrm(ks[16], (DEPTH, WIDTH_A, D_MODEL), WIDTH_A ** -0.5),
        'w_up_b': nrm(ks[17], (DEPTH, WIDTH_B, D_MODEL), WIDTH_B ** -0.5),
        'w_o': nrm(ks[18], (DEPTH, D_MODEL, D_MODEL), D_MODEL ** -0.5),
        'norm_moe_g': 1.0 + nrm(ks[19], (DEPTH, D_MODEL), 0.05),
        'w_router_group': nrm(ks[20], (DEPTH, D_MODEL, N_EXPERT_GROUPS), D_MODEL ** -0.5),
        'b_router_group': nrm(ks[21], (DEPTH, N_EXPERT_GROUPS), 0.01),
        'w_router_expert': nrm(ks[22], (DEPTH, D_MODEL, N_EXPERTS), D_MODEL ** -0.5),
        'b_router_expert': nrm(ks[23], (DEPTH, N_EXPERTS), 0.01),
        'w_exp_gate': nrm(ks[24], (DEPTH, N_EXPERTS, D_MODEL, D_FF_EXPERT), D_MODEL ** -0.5),
        'w_exp_up': nrm(ks[25], (DEPTH, N_EXPERTS, D_MODEL, D_FF_EXPERT), D_MODEL ** -0.5),
        'w_exp_down': nrm(ks[26], (DEPTH, N_EXPERTS, D_FF_EXPERT, D_MODEL), D_FF_EXPERT ** -0.5),
        'norm_ple_g': 1.0 + nrm(ks[27], (DEPTH, D_MODEL), 0.05),
        'w_ple_gate': nrm(ks[28], (DEPTH, D_MODEL, D_MODEL), D_MODEL ** -0.5),
        'b_ple_gate': nrm(ks[29], (DEPTH, D_MODEL), 0.02),
        'w_ple_proj': nrm(ks[30], (DEPTH, PLE_DIM, D_MODEL), PLE_DIM ** -0.5),
        'norm_ple_post_g': 1.0 + nrm(ks[31], (DEPTH, D_MODEL), 0.05),
        'norm_final_g': 1.0 + nrm(ks[32], (D_MODEL,), 0.05),
    }


def reference(x_prompt, x_sample, p_prompt, p_sample, cache_k, cache_v, cache_logf, page_table,
              norm_mix_g, w_in, b_f, ln_v_g, ln_v_b, w_spatial, b_spatial, w_up_a, w_up_b, w_o,
              norm_moe_g, w_router_group, b_router_group, w_router_expert, b_router_expert,
              w_exp_gate, w_exp_up, w_exp_down, norm_ple_g, w_ple_gate, b_ple_gate, w_ple_proj,
              norm_ple_post_g, norm_final_g):
    xp = x_prompt
    xs = x_sample
    kp_l, vp_l, lfp_l, ks_l, vs_l, lfs_l, cvs_l = [], [], [], [], [], [], []
    for i in range(DEPTH):
        hp = rmsnorm(xp, norm_mix_g[i])
        q, k, v, lf, u, vb, ga, gb = project_inputs(hp, w_in[i], b_f[i])
        a_out = fox_prompt(q, k, v, lf)
        b_out, _ = chunk_mlp(u, vb, ln_v_g[i], ln_v_b[i], w_spatial[i], b_spatial[i])
        xp = merge_branches(xp, a_out, b_out, ga, gb, w_up_a[i], w_up_b[i], w_o[i])
        kp_l.append(k)
        vp_l.append(v)
        lfp_l.append(lf)
        hs = rmsnorm(xs, norm_mix_g[i])
        q, k, v, lf, u, vb, ga, gb = project_inputs(hs, w_in[i], b_f[i])
        a_out = fox_sample(q, k, v, lf, cache_k[i], cache_v[i], cache_logf[i], page_table)
        b_out, v_rows = chunk_mlp(u, vb, ln_v_g[i], ln_v_b[i], w_spatial[i], b_spatial[i])
        xs = merge_branches(xs, a_out, b_out, ga, gb, w_up_a[i], w_up_b[i], w_o[i])
        ks_l.append(k)
        vs_l.append(v)
        lfs_l.append(lf)
        cvs_l.append(v_rows)
        xp = channel_and_ple(xp, p_prompt[i], norm_moe_g[i], w_router_group[i], b_router_group[i],
                             w_router_expert[i], b_router_expert[i], w_exp_gate[i], w_exp_up[i],
                             w_exp_down[i], norm_ple_g[i], w_ple_gate[i], b_ple_gate[i],
                             w_ple_proj[i], norm_ple_post_g[i])
        xs = channel_and_ple(xs, p_sample[i], norm_moe_g[i], w_router_group[i], b_router_group[i],
                             w_router_expert[i], b_router_expert[i], w_exp_gate[i], w_exp_up[i],
                             w_exp_down[i], norm_ple_g[i], w_ple_gate[i], b_ple_gate[i],
                             w_ple_proj[i], norm_ple_post_g[i])
    y_prompt = rmsnorm(xp, norm_final_g)
    y_sample = rmsnorm(xs, norm_final_g)
    return (y_prompt, y_sample, jnp.stack(kp_l), jnp.stack(vp_l), jnp.stack(lfp_l),
            jnp.stack(ks_l), jnp.stack(vs_l), jnp.stack(lfs_l), jnp.stack(cvs_l))
```

```python
import functools

import numpy as np
import jax
import jax.numpy as jnp
from jax import lax
from jax.experimental import pallas as pl
from jax.experimental.pallas import tpu as pltpu

F32 = jnp.float32
BF16 = jnp.bfloat16

D_MODEL = 1024
N_HEADS = 8
HEAD_DIM = 64
WIDTH_A = N_HEADS * HEAD_DIM
WIDTH_B = 512
N_GROUPS_B = 8
GROUP_DIM_B = WIDTH_B // N_GROUPS_B
CHUNK = 128
N_EXPERT_GROUPS = 4
EXPERTS_PER_GROUP = 8
N_EXPERTS = N_EXPERT_GROUPS * EXPERTS_PER_GROUP
D_FF = 512
PAGE = 128
EPS = 1e-6
QK_SCALE = HEAD_DIM ** -0.5

LANES = 128
TOKEN_TILE = 256
EXPERT_BLOCK = 256
ATTN_BLOCK = 256
PAGES_PER_STEP = 8
VMEM_LIMIT = 48 * 1024 * 1024

META_E0, META_E1, META_R0, META_R1, META_W0, META_W1 = 0, 1, 2, 3, 4, 5
ROUTER_EXPERT_LANE0 = N_EXPERT_GROUPS


def _const_spec(shape):
    nd = len(shape)
    return pl.BlockSpec(shape, lambda *args, _nd=nd: (0,) * _nd)


def _rmsnorm(x, g):
    return x * lax.rsqrt(jnp.mean(x * x, axis=-1, keepdims=True) + EPS) * g


def _split3(x):
    hi = x.astype(BF16)
    r = x - hi.astype(F32)
    mid = r.astype(BF16)
    lo = (r - mid.astype(F32)).astype(BF16)
    return hi, mid, lo


def _dot(a, b):
    return jnp.dot(a, b, preferred_element_type=F32)


def _dot_nt(a, b):
    return lax.dot_general(a, b, (((1,), (1,)), ((), ())), preferred_element_type=F32)


def _dot_split(a_bf16, x_f32):
    hi, mid, lo = _split3(x_f32)
    return _dot(a_bf16, hi) + _dot(a_bf16, mid) + _dot(a_bf16, lo)


def _inproj_kernel(x_ref, g_ref, w_ref, wf_ref, bf_ref, lng_ref, lnb_ref, cum_ref,
                   qb_ref, kb_ref, vb_ref, k_ref, v_ref, lf_ref, c_ref, ug_ref, vln_ref, ga_ref, gb_ref,
                   carry_ref, *, tiles_per_seq):
    i = pl.program_id(0)

    @pl.when(i % tiles_per_seq == 0)
    def _():
        carry_ref[...] = jnp.zeros_like(carry_ref)

    h = _rmsnorm(x_ref[...], g_ref[...]).astype(BF16)

    def proj(a, b):
        return _dot(h, w_ref[:, a:b])

    o = 0
    q = proj(o, o + WIDTH_A)
    qb_ref[...] = (q * QK_SCALE).astype(BF16)
    o += WIDTH_A
    k = proj(o, o + WIDTH_A)
    k_ref[...] = k
    kb_ref[...] = k.astype(BF16)
    o += WIDTH_A
    v = proj(o, o + WIDTH_A)
    v_ref[...] = v
    vb_ref[...] = v.astype(BF16)
    o += WIDTH_A
    ug_ref[...] = jax.nn.gelu(proj(o, o + WIDTH_B)).astype(ug_ref.dtype)
    o += WIDTH_B
    vg = jax.nn.gelu(proj(o, o + WIDTH_B))
    o += WIDTH_B
    mu = jnp.mean(vg, axis=-1, keepdims=True)
    vc = vg - mu
    vln = vc * lax.rsqrt(jnp.mean(vc * vc, axis=-1, keepdims=True) + EPS) * lng_ref[...] + lnb_ref[...]
    vln_ref[...] = vln.astype(vln_ref.dtype)
    ga_ref[...] = jax.nn.sigmoid(proj(o, o + D_MODEL)).astype(BF16)
    o += D_MODEL
    gb_ref[...] = jax.nn.sigmoid(proj(o, o + D_MODEL)).astype(BF16)

    f = _dot(h, wf_ref[...]) + bf_ref[...]
    lf = jnp.minimum(f, 0.0) - jnp.log1p(jnp.exp(-jnp.abs(f)))
    c = _dot_split(cum_ref[...], lf) + carry_ref[...]
    carry_ref[...] = c[c.shape[0] - 1:, :]
    lf_ref[...] = lf[:, :N_HEADS]
    c_ref[...] = c[:, :N_HEADS]


def _inproj(x2d, g, w_main, w_f, b_f, ln_g, ln_b, cum, *, tiles_per_seq, vln_dtype):
    T = x2d.shape[0]
    tm = TOKEN_TILE
    nt = T // tm
    row = lambda n: pl.BlockSpec((tm, n), lambda i: (i, 0))
    out_shape = (
        jax.ShapeDtypeStruct((T, WIDTH_A), BF16),
        jax.ShapeDtypeStruct((T, WIDTH_A), BF16),
        jax.ShapeDtypeStruct((T, WIDTH_A), BF16),
        jax.ShapeDtypeStruct((T, WIDTH_A), F32),
        jax.ShapeDtypeStruct((T, WIDTH_A), F32),
        jax.ShapeDtypeStruct((T, N_HEADS), F32),
        jax.ShapeDtypeStruct((T, N_HEADS), F32),
        jax.ShapeDtypeStruct((T, WIDTH_B), BF16),
        jax.ShapeDtypeStruct((T, WIDTH_B), vln_dtype),
        jax.ShapeDtypeStruct((T, D_MODEL), BF16),
        jax.ShapeDtypeStruct((T, D_MODEL), BF16),
    )
    out_specs = (row(WIDTH_A), row(WIDTH_A), row(WIDTH_A), row(WIDTH_A), row(WIDTH_A),
                 row(N_HEADS), row(N_HEADS), row(WIDTH_B), row(WIDTH_B), row(D_MODEL), row(D_MODEL))
    return pl.pallas_call(
        functools.partial(_inproj_kernel, tiles_per_seq=tiles_per_seq),
        grid=(nt,),
        in_specs=[row(D_MODEL), _const_spec(g.shape), _const_spec(w_main.shape), _const_spec(w_f.shape),
                  _const_spec(b_f.shape), _const_spec(ln_g.shape), _const_spec(ln_b.shape), _const_spec(cum.shape)],
        out_specs=out_specs,
        out_shape=out_shape,
        scratch_shapes=[pltpu.VMEM((1, LANES), F32)],
        compiler_params=pltpu.CompilerParams(dimension_semantics=("arbitrary",), vmem_limit_bytes=VMEM_LIMIT),
        name="inproj",
    )(x2d, g, w_main, w_f, b_f, ln_g, ln_b, cum)


def _fox_prompt_kernel(q_ref, k_ref, v_ref, ct_ref, o_ref, *, blk):
    hp = pl.program_id(1)
    i = pl.program_id(2)
    q = q_ref[0]
    lane = lax.broadcasted_iota(jnp.int32, q.shape, 1)
    zero = jnp.zeros_like(q)
    qh = (jnp.where(lane < HEAD_DIM, q, zero), jnp.where(lane >= HEAD_DIM, q, zero))
    r_io = lax.broadcasted_iota(jnp.int32, (blk, blk), 0)
    c_io = lax.broadcasted_iota(jnp.int32, (blk, blk), 1)

    def step(j, carry, masked):
        off = pl.multiple_of(j * blk, blk)
        kj = k_ref[0, pl.ds(off, blk), :]
        vj = v_ref[0, pl.ds(off, blk), :]
        new = []
        for h in range(2):
            m, l, acc = carry[h]
            crow = ct_ref[0, 2 * hp + h, pl.ds(j, 1), :]
            s = _dot_nt(qh[h], kj) - crow
            if masked:
                s = jnp.where(r_io >= c_io, s, -jnp.inf)
            m_new = jnp.maximum(m, jnp.max(s, axis=-1, keepdims=True))
            alpha = jnp.exp(m - m_new)
            p = jnp.exp(s - m_new)
            l = alpha * l + jnp.sum(p, axis=-1, keepdims=True)
            acc = alpha * acc + _dot(p.astype(BF16), vj)
            new.append((m_new, l, acc))
        return tuple(new)

    init = tuple((jnp.full((blk, 1), -jnp.inf, F32), jnp.zeros((blk, 1), F32), jnp.zeros((blk, 2 * HEAD_DIM), F32))
                 for _ in range(2))
    carry = lax.fori_loop(0, i, lambda j, c: step(j, c, False), init)
    (m0, l0, a0), (m1, l1, a1) = step(i, carry, True)
    o_ref[0] = jnp.where(lane < HEAD_DIM, a0 / l0, a1 / l1).astype(o_ref.dtype)


def _fox_prompt(qb, kb, vb, ct):
    B, S, _ = qb.shape
    blk = min(ATTN_BLOCK, S)
    nq = S // blk
    ct4 = ct.reshape(B, N_HEADS, nq, blk)
    return pl.pallas_call(
        functools.partial(_fox_prompt_kernel, blk=blk),
        grid=(B, N_HEADS // 2, nq),
        in_specs=[pl.BlockSpec((1, blk, 2 * HEAD_DIM), lambda b, hp, i: (b, i, hp)),
                  pl.BlockSpec((1, S, 2 * HEAD_DIM), lambda b, hp, i: (b, 0, hp)),
                  pl.BlockSpec((1, S, 2 * HEAD_DIM), lambda b, hp, i: (b, 0, hp)),
                  pl.BlockSpec((1, N_HEADS, nq, blk), lambda b, hp, i: (b, 0, 0, 0))],
        out_specs=pl.BlockSpec((1, blk, 2 * HEAD_DIM), lambda b, hp, i: (b, i, hp)),
        out_shape=jax.ShapeDtypeStruct((B, S, WIDTH_A), BF16),
        compiler_params=pltpu.CompilerParams(dimension_semantics=("arbitrary",) * 3, vmem_limit_bytes=VMEM_LIMIT),
        name="fox_prompt",
    )(qb, kb, vb, ct4)


def _suffix_matrices():
    l = np.arange(LANES)
    col = np.arange(N_HEADS * PAGE)
    h_c, s_c = col // PAGE, col % PAGE
    same_head = (l[:, None] % N_HEADS) == h_c[None, :]
    per_row = PAGE // 8
    later = (l[:, None] // N_HEADS) > (s_c[None, :] % per_row)
    return np.concatenate([same_head & later, same_head], axis=1).astype(np.float32)


def _fox_sample_kernel(pt_ref, *refs, n_pages_step):
    G = n_pages_step
    k_refs = refs[0:G]
    v_refs = refs[G:2 * G]
    lf_refs = refs[2 * G:3 * G]
    q_ref, kn_ref, vn_ref, cnt_ref, sm_ref, o_ref, m_ref, l_ref, acc_ref, carry_ref = refs[3 * G:]
    jj = pl.program_id(1)
    n_steps = pl.num_programs(1)
    T = q_ref.shape[1]
    R = N_HEADS * T

    q = q_ref[0]
    lane_head = lax.broadcasted_iota(jnp.int32, q.shape, 1) // HEAD_DIM
    qbd = jnp.concatenate([jnp.where(lane_head == h, q, jnp.zeros_like(q)) for h in range(N_HEADS)], axis=0)

    def head_rows(row_by_head):
        return jnp.concatenate(
            [jnp.broadcast_to(row_by_head[:, h * PAGE:(h + 1) * PAGE], (T, PAGE)) for h in range(N_HEADS)], axis=0)

    def accumulate(s_list, v_list):
        s = jnp.concatenate(s_list, axis=1) if len(s_list) > 1 else s_list[0]
        m = m_ref[...]
        m_new = jnp.maximum(m, jnp.max(s, axis=-1, keepdims=True))
        alpha = jnp.exp(m - m_new)
        p = jnp.exp(s - m_new)
        l_ref[...] = alpha * l_ref[...] + jnp.sum(p, axis=-1, keepdims=True)
        acc = alpha * acc_ref[...]
        for g, vg in enumerate(v_list):
            acc = acc + _dot(p[:, g * PAGE:(g + 1) * PAGE].astype(BF16), vg)
        acc_ref[...] = acc
        m_ref[...] = m_new

    @pl.when(jj == 0)
    def _():
        carry_ref[...] = jnp.zeros_like(carry_ref)
        m_ref[...] = jnp.full(m_ref.shape, -jnp.inf, F32)
        l_ref[...] = jnp.zeros_like(l_ref)
        acc_ref[...] = jnp.zeros_like(acc_ref)
        pad = jnp.zeros((PAGE - T, WIDTH_A), BF16)
        kn = jnp.concatenate([kn_ref[0].astype(BF16), pad], axis=0)
        vn = jnp.concatenate([vn_ref[0].astype(BF16), pad], axis=0)
        cn = cnt_ref[0]
        bias = jnp.concatenate([jnp.broadcast_to(cn[h:h + 1, :], (T, PAGE)) for h in range(N_HEADS)], axis=0)
        s = _dot_nt(qbd, kn) - bias
        t_row = lax.broadcasted_iota(jnp.int32, (R, PAGE), 0) % T
        s_col = lax.broadcasted_iota(jnp.int32, (R, PAGE), 1)
        s = jnp.where(s_col <= t_row, s, -jnp.inf)
        accumulate([s], [vn])

    x = jnp.concatenate([r[0] for r in lf_refs], axis=0)
    hi, mid, lo = _split3(x)
    sm = sm_ref[...]
    y = _dot(hi, sm) + _dot(mid, sm) + _dot(lo, sm)
    HP = N_HEADS * PAGE
    r_io = lax.broadcasted_iota(jnp.int32, (8, HP), 0)
    s_blk = (lax.broadcasted_iota(jnp.int32, (8, HP), 1) % PAGE) // (PAGE // 8)
    carry = carry_ref[...]
    biases = [None] * G
    for g in reversed(range(G)):
        y1 = y[g * 8:(g + 1) * 8, :HP]
        y2 = y[g * 8:(g + 1) * 8, HP:]
        within = jnp.where(r_io == s_blk, y1, 0.0) + jnp.where(r_io > s_blk, y2, 0.0)
        biases[g] = head_rows(jnp.sum(within, axis=0, keepdims=True) + carry)
        carry = carry + jnp.sum(y2, axis=0, keepdims=True)
    carry_ref[...] = carry

    s_list = [_dot_nt(qbd, k_refs[g][0].astype(BF16)) + biases[g] for g in range(G)]
    accumulate(s_list, [v_refs[g][0].astype(BF16) for g in range(G)])

    @pl.when(jj == n_steps - 1)
    def _():
        out = acc_ref[...] / l_ref[...]
        res = jnp.zeros((T, WIDTH_A), F32)
        for h in range(N_HEADS):
            res = res + jnp.where(lane_head == h, out[h * T:(h + 1) * T, :], 0.0)
        o_ref[0] = res.astype(o_ref.dtype)


def _fox_sample(qb, k_new, v_new, cn_t, cache_k, cache_v, cache_logf, page_table):
    DB, T, _ = qb.shape
    n_pages = page_table.shape[1]
    G = min(PAGES_PER_STEP, n_pages)
    n_steps = n_pages // G
    sm = jnp.asarray(_suffix_matrices(), BF16)

    def page_spec(shape, g):
        return pl.BlockSpec(shape, lambda b, jj, pt, _g=g: (pt[b, (n_steps - 1 - jj) * G + _g], 0, 0))

    per_b = lambda shape: pl.BlockSpec(shape, lambda b, jj, pt: (b, 0, 0))
    in_specs = ([page_spec((1, PAGE, WIDTH_A), g) for g in range(G)]
                + [page_spec((1, PAGE, WIDTH_A), g) for g in range(G)]
                + [page_spec((1, 8, LANES), g) for g in range(G)]
                + [per_b((1, T, WIDTH_A)), per_b((1, T, WIDTH_A)), per_b((1, T, WIDTH_A)), per_b((1, N_HEADS, LANES)),
                   pl.BlockSpec(sm.shape, lambda b, jj, pt: (0, 0))])
    grid_spec = pltpu.PrefetchScalarGridSpec(
        num_scalar_prefetch=1,
        grid=(DB, n_steps),
        in_specs=in_specs,
        out_specs=pl.BlockSpec((1, T, WIDTH_A), lambda b, jj, pt: (b, 0, 0)),
        scratch_shapes=[pltpu.VMEM((N_HEADS * T, 1), F32), pltpu.VMEM((N_HEADS * T, 1), F32),
                        pltpu.VMEM((N_HEADS * T, WIDTH_A), F32), pltpu.VMEM((1, N_HEADS * PAGE), F32)],
    )
    return pl.pallas_call(
        functools.partial(_fox_sample_kernel, n_pages_step=G),
        grid_spec=grid_spec,
        out_shape=jax.ShapeDtypeStruct((DB, T, WIDTH_A), BF16),
        compiler_params=pltpu.CompilerParams(dimension_semantics=("arbitrary", "arbitrary"),
                                             vmem_limit_bytes=VMEM_LIMIT),
        name="fox_sample",
    )(page_table, *([cache_k] * G), *([cache_v] * G), *([cache_logf] * G), qb, k_new, v_new, cn_t, sm)


def _merge_kernel(x_ref, a_ref, ug_ref, vln_ref, ga_ref, gb_ref, wsp_ref, bsp_ref, wua_ref, wub_ref, wo_ref,
                  gm_ref, wr_ref, br_ref, tri_ref,
                  x1_ref, h2_ref, meta_ref, cnt_ref, carry_ref, *, chunk_rows):
    i = pl.program_id(0)
    tm = x_ref.shape[0]

    @pl.when(i == 0)
    def _():
        carry_ref[...] = jnp.zeros_like(carry_ref)

    col_group = lax.broadcasted_iota(jnp.int32, (chunk_rows, WIDTH_B), 1) // GROUP_DIM_B
    sv_chunks = []
    for c in range(tm // chunk_rows):
        vc = vln_ref[c * chunk_rows:(c + 1) * chunk_rows, :].astype(BF16)
        sv = bsp_ref[...]
        for g in range(N_GROUPS_B):
            sv = sv + jnp.where(col_group == g, _dot(wsp_ref[g], vc), 0.0)
        sv_chunks.append(sv)
    sv = jnp.concatenate(sv_chunks, axis=0) if len(sv_chunks) > 1 else sv_chunks[0]
    b_out = (ug_ref[...].astype(F32) * sv).astype(BF16)

    m = (ga_ref[...].astype(F32) * _dot(a_ref[...], wua_ref[...])
         + gb_ref[...].astype(F32) * _dot(b_out, wub_ref[...]))
    x1 = x_ref[...] + _dot(m.astype(BF16), wo_ref[...])
    x1_ref[...] = x1

    h2 = _rmsnorm(x1, gm_ref[...])
    h2_ref[...] = h2
    logits = jnp.dot(h2, wr_ref[...], preferred_element_type=F32, precision=lax.Precision.HIGHEST) + br_ref[...]
    lane = lax.broadcasted_iota(jnp.int32, logits.shape, 1)
    lane_f = lane.astype(F32)
    big = jnp.float32(LANES)

    def first_lane_where(cond):
        return jnp.min(jnp.where(cond, lane_f, big), axis=-1, keepdims=True)

    gmask = lane < N_EXPERT_GROUPS
    gl = jnp.where(gmask, logits, -jnp.inf)
    gmax = jnp.max(gl, axis=-1, keepdims=True)
    gsel = first_lane_where(gl == gmax)
    pg = 1.0 / jnp.sum(jnp.exp(gl - gmax), axis=-1, keepdims=True)

    lo = ROUTER_EXPERT_LANE0 + EXPERTS_PER_GROUP * gsel
    emask = jnp.logical_and(lane_f >= lo, lane_f < lo + EXPERTS_PER_GROUP)
    el = jnp.where(emask, logits, -jnp.inf)
    emax = jnp.max(el, axis=-1, keepdims=True)
    ex = jnp.exp(el - emax)
    eprob = ex / jnp.sum(ex, axis=-1, keepdims=True)
    ep1 = jnp.where(emask, eprob, -1.0)
    v1 = jnp.max(ep1, axis=-1, keepdims=True)
    i1 = first_lane_where(ep1 == v1)
    ep2 = jnp.where(lane_f == i1, -1.0, ep1)
    v2 = jnp.max(ep2, axis=-1, keepdims=True)
    i2 = first_lane_where(ep2 == v2)
    w1 = pg * (v1 / (v1 + v2))
    w2 = pg * (v2 / (v1 + v2))

    sel1 = lane_f == i1
    sel2 = lane_f == i2
    onehot = jnp.where(jnp.logical_or(sel1, sel2), 1.0, 0.0)
    before = _dot(tri_ref[...], onehot.astype(BF16)) + carry_ref[...]
    r1 = jnp.sum(jnp.where(sel1, before, 0.0), axis=-1, keepdims=True)
    r2 = jnp.sum(jnp.where(sel2, before, 0.0), axis=-1, keepdims=True)
    carry_ref[...] = carry_ref[...] + jnp.sum(onehot, axis=0, keepdims=True)
    cnt_ref[...] = carry_ref[...]

    meta = jnp.zeros(logits.shape, F32)
    for ln, val in ((META_E0, i1 - ROUTER_EXPERT_LANE0), (META_E1, i2 - ROUTER_EXPERT_LANE0),
                    (META_R0, r1), (META_R1, r2), (META_W0, w1), (META_W1, w2)):
        meta = jnp.where(lane == ln, val, meta)
    meta_ref[...] = meta


def _merge(x2d, a, ug, vln, ga, gb, wsp, bsp, wua, wub, wo, gm, wr, br, tri):
    T = x2d.shape[0]
    tm = TOKEN_TILE
    row = lambda n: pl.BlockSpec((tm, n), lambda i: (i, 0))
    consts = (wsp, bsp, wua, wub, wo, gm, wr, br, tri)
    return pl.pallas_call(
        functools.partial(_merge_kernel, chunk_rows=wsp.shape[1]),
        grid=(T // tm,),
        in_specs=[row(D_MODEL), row(WIDTH_A), row(WIDTH_B), row(WIDTH_B), row(D_MODEL), row(D_MODEL)]
                 + [_const_spec(c.shape) for c in consts],
        out_specs=(row(D_MODEL), row(D_MODEL), row(LANES), _const_spec((1, LANES))),
        out_shape=(jax.ShapeDtypeStruct((T, D_MODEL), F32), jax.ShapeDtypeStruct((T, D_MODEL), F32),
                   jax.ShapeDtypeStruct((T, LANES), F32), jax.ShapeDtypeStruct((1, LANES), F32)),
        scratch_shapes=[pltpu.VMEM((1, LANES), F32)],
        compiler_params=pltpu.CompilerParams(dimension_semantics=("arbitrary",), vmem_limit_bytes=VMEM_LIMIT),
        name="merge_route",
    )(x2d, a, ug, vln, ga, gb, *consts)


def _dispatch_kernel(dest_ref, h_ref, xb_in_ref, xb_ref, sem):
    del xb_in_ref
    tm = h_ref.shape[0]

    def row_copy(r, k):
        return pltpu.make_async_copy(h_ref.at[pl.ds(r, 1), :], xb_ref.at[pl.ds(dest_ref[0, 0, 2 * r + k], 1), :], sem)

    def issue(r, _):
        row_copy(r, 0).start()
        row_copy(r, 1).start()
        return 0

    def drain(r, _):
        row_copy(r, 0).wait()
        row_copy(r, 1).wait()
        return 0

    lax.fori_loop(0, tm, issue, 0)
    lax.fori_loop(0, tm, drain, 0)


def _dispatch(h2, dest, xb):
    T = h2.shape[0]
    tm = TOKEN_TILE
    nt = T // tm
    dest3 = dest.reshape(nt, 1, 2 * tm)
    return pl.pallas_call(
        _dispatch_kernel,
        grid=(nt,),
        in_specs=[pl.BlockSpec((1, 1, 2 * tm), lambda i: (i, 0, 0), memory_space=pltpu.SMEM),
                  pl.BlockSpec((tm, D_MODEL), lambda i: (i, 0)),
                  pl.BlockSpec(memory_space=pl.ANY)],
        out_specs=pl.BlockSpec(memory_space=pl.ANY),
        out_shape=jax.ShapeDtypeStruct(xb.shape, xb.dtype),
        scratch_shapes=[pltpu.SemaphoreType.DMA(())],
        input_output_aliases={2: 0},
        compiler_params=pltpu.CompilerParams(dimension_semantics=("arbitrary",), has_side_effects=True),
        name="dispatch",
    )(dest3, h2, xb)


def _experts_kernel(be_ref, nu_ref, xb_ref, wg_ref, wu_ref, wd_ref, yb_ref):
    b = pl.program_id(0)

    @pl.when(b < nu_ref[0])
    def _():
        x = xb_ref[...].astype(BF16)
        hmid = jax.nn.silu(_dot(x, wg_ref[0])) * _dot(x, wu_ref[0])
        yb_ref[...] = _dot(hmid.astype(BF16), wd_ref[0])

    @pl.when(b >= nu_ref[0])
    def _():
        yb_ref[...] = jnp.zeros_like(yb_ref)


def _experts(xb, block_e, n_used, wg, wu, wd):
    P = xb.shape[0]
    bm = EXPERT_BLOCK
    nb = P // bm
    blk = lambda b, be, nu: (jnp.minimum(b, nu[0] - 1), 0)
    wsel = lambda b, be, nu: (be[jnp.minimum(b, nu[0] - 1)], 0, 0)
    grid_spec = pltpu.PrefetchScalarGridSpec(
        num_scalar_prefetch=2,
        grid=(nb,),
        in_specs=[pl.BlockSpec((bm, D_MODEL), blk),
                  pl.BlockSpec((1, D_MODEL, D_FF), wsel),
                  pl.BlockSpec((1, D_MODEL, D_FF), wsel),
                  pl.BlockSpec((1, D_FF, D_MODEL), wsel)],
        out_specs=pl.BlockSpec((bm, D_MODEL), lambda b, be, nu: (b, 0)),
    )
    return pl.pallas_call(
        _experts_kernel,
        grid_spec=grid_spec,
        out_shape=jax.ShapeDtypeStruct((P, D_MODEL), F32),
        compiler_params=pltpu.CompilerParams(dimension_semantics=("arbitrary",), vmem_limit_bytes=VMEM_LIMIT),
        name="experts",
    )(block_e, n_used, xb, wg, wu, wd)


def _combine_kernel(dest_ref, x1_ref, meta_ref, p_ref, yb_ref, gp_ref, wpg_ref, bpg_ref, wpp_ref, gpp_ref, gf_ref,
                    y_ref, buf_ref, sem):
    tm = x1_ref.shape[0]

    def row_copy(r, k):
        return pltpu.make_async_copy(yb_ref.at[pl.ds(dest_ref[0, 0, 2 * r + k], 1), :],
                                     buf_ref.at[k, pl.ds(r, 1), :], sem)

    def issue(r, _):
        row_copy(r, 0).start()
        row_copy(r, 1).start()
        return 0

    def drain(r, _):
        row_copy(r, 0).wait()
        row_copy(r, 1).wait()
        return 0

    lax.fori_loop(0, tm, issue, 0)
    e = _rmsnorm(_dot(p_ref[...].astype(BF16), wpp_ref[...]), gpp_ref[...])
    lax.fori_loop(0, tm, drain, 0)

    meta = meta_ref[...]
    w0 = meta[:, META_W0:META_W0 + 1]
    w1 = meta[:, META_W1:META_W1 + 1]
    x2 = x1_ref[...] + (buf_ref[0] * w0 + buf_ref[1] * w1)
    gate = jax.nn.sigmoid(_dot(_rmsnorm(x2, gp_ref[...]).astype(BF16), wpg_ref[...]) + bpg_ref[...])
    x3 = x2 + gate * e
    y_ref[...] = _rmsnorm(x3, gf_ref[...])


def _combine(x1, meta, p2d, dest, yb, gp, wpg, bpg, wpp, gpp, gf):
    T = x1.shape[0]
    tm = TOKEN_TILE
    nt = T // tm
    dest3 = dest.reshape(nt, 1, 2 * tm)
    row = lambda n: pl.BlockSpec((tm, n), lambda i: (i, 0))
    consts = (gp, wpg, bpg, wpp, gpp, gf)
    return pl.pallas_call(
        _combine_kernel,
        grid=(nt,),
        in_specs=[pl.BlockSpec((1, 1, 2 * tm), lambda i: (i, 0, 0), memory_space=pltpu.SMEM),
                  row(D_MODEL), row(LANES), row(p2d.shape[1]), pl.BlockSpec(memory_space=pl.ANY)]
                 + [_const_spec(c.shape) for c in consts],
        out_specs=row(D_MODEL),
        out_shape=jax.ShapeDtypeStruct((T, D_MODEL), F32),
        scratch_shapes=[pltpu.VMEM((2, tm, D_MODEL), F32), pltpu.SemaphoreType.DMA(())],
        compiler_params=pltpu.CompilerParams(dimension_semantics=("arbitrary",), vmem_limit_bytes=VMEM_LIMIT),
        name="combine_ple",
    )(dest3, x1, meta, p2d, yb, *consts)


def _cum_matrix(tm, seq_len):
    r = np.arange(tm)
    m = r[:, None] >= r[None, :]
    if seq_len < tm:
        m &= (r[:, None] // seq_len) == (r[None, :] // seq_len)
    return m.astype(np.float32)


def _layer(xp, xs, pp, ps, cache_k, cache_v, cache_logf, page_table, norm_mix_g, w_in, b_f, ln_v_g, ln_v_b,
           w_spatial, b_spatial, w_up_a, w_up_b, w_o, norm_moe_g, w_rg, b_rg, w_re, b_re, w_eg, w_eu, w_ed,
           norm_ple_g, w_ple_gate, b_ple_gate, w_ple_proj, norm_ple_post_g, norm_final_g):
    B, S, _ = xp.shape
    DB, T, _ = xs.shape
    tm = TOKEN_TILE
    Tp, Ts = B * S, DB * T
    assert Tp % tm == 0 and Ts % tm == 0 and tm % T == 0 and S % tm == 0

    off_f = 3 * WIDTH_A
    w_main = jnp.concatenate([w_in[:, :off_f], w_in[:, off_f + N_HEADS:]], axis=1).astype(BF16)
    w_f = jnp.pad(w_in[:, off_f:off_f + N_HEADS], ((0, 0), (0, LANES - N_HEADS))).astype(BF16)
    b_f128 = jnp.pad(b_f, (0, LANES - N_HEADS)).reshape(1, LANES)
    row1 = lambda v: v.reshape(1, -1)
    w_router = jnp.pad(jnp.concatenate([w_rg, w_re], axis=1), ((0, 0), (0, LANES - N_EXPERT_GROUPS - N_EXPERTS)))
    b_router = jnp.pad(jnp.concatenate([b_rg, b_re]), (0, LANES - N_EXPERT_GROUPS - N_EXPERTS)).reshape(1, LANES)
    tri = jnp.asarray(np.tril(np.ones((tm, tm), np.float32), -1), BF16)

    def spatial_params(seq_len):
        L = min(seq_len, CHUNK)
        w = w_spatial[:, :L, :L] * jnp.tril(jnp.ones((L, L), w_spatial.dtype))
        bias = jnp.repeat(b_spatial[:, :L].T, GROUP_DIM_B, axis=1)
        if L < CHUNK:
            reps = tm // L
            eye = jnp.eye(reps, dtype=w.dtype)
            w = jnp.einsum('ab,gts->gatbs', eye, w).reshape(N_GROUPS_B, tm, tm)
            bias = jnp.tile(bias, (reps, 1))
        return w.astype(BF16), bias

    common = (row1(norm_mix_g), w_main, w_f, b_f128, row1(ln_v_g), row1(ln_v_b))
    merge_w = (w_up_a.astype(BF16), w_up_b.astype(BF16), w_o.astype(BF16), row1(norm_moe_g), w_router, b_router, tri)

    cum_p = jnp.asarray(_cum_matrix(tm, S), BF16)
    qb, kb, vb, k_p, v_p, lf_p, c_p, ug_p, vln_p, ga_p, gb_p = _inproj(
        xp.reshape(Tp, D_MODEL), *common, cum_p, tiles_per_seq=S // tm, vln_dtype=BF16)
    ct = c_p.reshape(B, S, N_HEADS).transpose(0, 2, 1)
    a_p = _fox_prompt(qb.reshape(B, S, WIDTH_A), kb.reshape(B, S, WIDTH_A), vb.reshape(B, S, WIDTH_A), ct)
    wsp_p, bsp_p = spatial_params(S)
    x1_p, h2_p, meta_p, cnt_p = _merge(xp.reshape(Tp, D_MODEL), a_p.reshape(Tp, WIDTH_A), ug_p, vln_p, ga_p, gb_p,
                                       wsp_p, bsp_p, *merge_w)

    cum_s = jnp.asarray(_cum_matrix(tm, T), BF16)
    qs, _, _, k_s, v_s, lf_s, c_s, ug_s, vln_s, ga_s, gb_s = _inproj(
        xs.reshape(Ts, D_MODEL), *common, cum_s, tiles_per_seq=1, vln_dtype=F32)
    cn_t = jnp.pad(c_s.reshape(DB, T, N_HEADS).transpose(0, 2, 1), ((0, 0), (0, 0), (0, LANES - T)))
    n_pool = cache_k.shape[0]
    a_s = _fox_sample(qs.reshape(DB, T, WIDTH_A), k_s.reshape(DB, T, WIDTH_A), v_s.reshape(DB, T, WIDTH_A), cn_t,
                      cache_k.reshape(n_pool, PAGE, WIDTH_A), cache_v.reshape(n_pool, PAGE, WIDTH_A),
                      cache_logf.reshape(n_pool, 8, LANES), page_table)
    wsp_s, bsp_s = spatial_params(T)
    x1_s, h2_s, meta_s, cnt_s = _merge(xs.reshape(Ts, D_MODEL), a_s.reshape(Ts, WIDTH_A), ug_s, vln_s, ga_s, gb_s,
                                       wsp_s, bsp_s, *merge_w)

    bm = EXPERT_BLOCK
    e0 = ROUTER_EXPERT_LANE0
    counts_p = cnt_p[0, e0:e0 + N_EXPERTS].astype(jnp.int32)
    counts_s = cnt_s[0, e0:e0 + N_EXPERTS].astype(jnp.int32)
    counts = counts_p + counts_s
    padded = (counts + bm - 1) // bm * bm
    pend = jnp.cumsum(padded)
    pstart = pend - padded
    n_assign = 2 * (Tp + Ts)
    nb = -(-(n_assign + N_EXPERTS * (bm - 1)) // bm)
    block_e = jnp.minimum(jnp.searchsorted(pend, jnp.arange(nb, dtype=jnp.int32) * bm, side='right'),
                          N_EXPERTS - 1).astype(jnp.int32)
    n_used = (pend[-1:] // bm).astype(jnp.int32)

    def slots(meta, base):
        e = meta[:, META_E0:META_E1 + 1].astype(jnp.int32)
        r = meta[:, META_R0:META_R1 + 1].astype(jnp.int32)
        return base[e] + r

    dest_p = slots(meta_p, pstart)
    dest_s = slots(meta_s, pstart + counts_p)

    xb = jnp.zeros((nb * bm, D_MODEL), F32)
    xb = _dispatch(h2_p, dest_p, xb)
    xb = _dispatch(h2_s, dest_s, xb)
    yb = _experts(xb, block_e, n_used, w_eg.astype(BF16), w_eu.astype(BF16), w_ed.astype(BF16))

    ple_w = (row1(norm_ple_g), w_ple_gate.astype(BF16), row1(b_ple_gate), w_ple_proj.astype(BF16),
             row1(norm_ple_post_g), row1(norm_final_g))
    y_p = _combine(x1_p, meta_p, pp.reshape(Tp, -1), dest_p, yb, *ple_w)
    y_s = _combine(x1_s, meta_s, ps.reshape(Ts, -1), dest_s, yb, *ple_w)

    return (y_p.reshape(B, S, D_MODEL), y_s.reshape(DB, T, D_MODEL),
            k_p.reshape(B, S, N_HEADS, HEAD_DIM), v_p.reshape(B, S, N_HEADS, HEAD_DIM), lf_p.reshape(B, S, N_HEADS),
            k_s.reshape(DB, T, N_HEADS, HEAD_DIM), v_s.reshape(DB, T, N_HEADS, HEAD_DIM), lf_s.reshape(DB, T, N_HEADS),
            vln_s.reshape(DB, T, WIDTH_B))


def kernel(x_prompt, x_sample, p_prompt, p_sample, cache_k, cache_v, cache_logf, page_table, norm_mix_g, w_in, b_f, ln_v_g, ln_v_b, w_spatial, b_spatial, w_up_a, w_up_b, w_o, norm_moe_g, w_router_group, b_router_group, w_router_expert, b_router_expert, w_exp_gate, w_exp_up, w_exp_down, norm_ple_g, w_ple_gate, b_ple_gate, w_ple_proj, norm_ple_post_g, norm_final_g):
    depth = w_in.shape[0]
    assert depth == 1, "the final norm is fused into the layer's last kernel"
    outs = _layer(x_prompt, x_sample, p_prompt[0], p_sample[0], cache_k[0], cache_v[0], cache_logf[0], page_table,
                  norm_mix_g[0], w_in[0], b_f[0], ln_v_g[0], ln_v_b[0], w_spatial[0], b_spatial[0], w_up_a[0],
                  w_up_b[0], w_o[0], norm_moe_g[0], w_router_group[0], b_router_group[0], w_router_expert[0],
                  b_router_expert[0], w_exp_gate[0], w_exp_up[0], w_exp_down[0], norm_ple_g[0], w_ple_gate[0],
                  b_ple_gate[0], w_ple_proj[0], norm_ple_post_g[0], norm_final_g)
    y_p, y_s = outs[0], outs[1]
    return (y_p, y_s) + tuple(o[None] for o in outs[2:])
```

```python
import functools

import numpy as np
import jax
import jax.numpy as jnp
from jax import lax
from jax.experimental import pallas as pl
from jax.experimental.pallas import tpu as pltpu

F32 = jnp.float32
BF16 = jnp.bfloat16

D_MODEL = 1024
N_HEADS = 8
HEAD_DIM = 64
WIDTH_A = N_HEADS * HEAD_DIM
WIDTH_B = 512
N_GROUPS_B = 8
GROUP_DIM_B = WIDTH_B // N_GROUPS_B
CHUNK = 128
N_EXPERT_GROUPS = 4
EXPERTS_PER_GROUP = 8
N_EXPERTS = N_EXPERT_GROUPS * EXPERTS_PER_GROUP
D_FF = 512
PAGE = 128
EPS = 1e-6
LOG2E = 1.4426950408889634
QK_SCALE = HEAD_DIM ** -0.5 * LOG2E

LANES = 128
TOKEN_TILE = 256
EXPERT_BLOCK = 256
ATTN_BLOCK = 512
ATTN_HEADS_PER_STEP = 2
PAGES_PER_STEP = 8
VMEM_LIMIT = 48 * 1024 * 1024

META_E0, META_E1, META_R0, META_R1, META_W0, META_W1 = 0, 1, 2, 3, 4, 5
ROUTER_EXPERT_LANE0 = N_EXPERT_GROUPS


def _const_spec(shape):
    nd = len(shape)
    return pl.BlockSpec(shape, lambda *args, _nd=nd: (0,) * _nd)


def _rmsnorm(x, g):
    return x * lax.rsqrt(jnp.mean(x * x, axis=-1, keepdims=True) + EPS) * g


def _split3(x):
    hi = x.astype(BF16)
    r = x - hi.astype(F32)
    mid = r.astype(BF16)
    lo = (r - mid.astype(F32)).astype(BF16)
    return hi, mid, lo


def _dot(a, b):
    return jnp.dot(a, b, preferred_element_type=F32)


def _dot_nt(a, b):
    return lax.dot_general(a, b, (((1,), (1,)), ((), ())), preferred_element_type=F32)


def _dot_split(a_bf16, x_f32):
    hi, mid, lo = _split3(x_f32)
    return _dot(a_bf16, hi) + _dot(a_bf16, mid) + _dot(a_bf16, lo)


def _inproj_kernel(x_ref, g_ref, w_ref, wf_ref, bf_ref, lng_ref, lnb_ref, cum_ref,
                   qb_ref, kb_ref, vb_ref, k_ref, v_ref, lf_ref, c_ref, ug_ref, vln_ref, ga_ref, gb_ref,
                   carry_ref, *, tiles_per_seq):
    i = pl.program_id(0)

    @pl.when(i % tiles_per_seq == 0)
    def _():
        carry_ref[...] = jnp.zeros_like(carry_ref)

    h = _rmsnorm(x_ref[...], g_ref[...]).astype(BF16)

    def proj(a, b):
        return _dot(h, w_ref[:, a:b])

    o = 0
    q = proj(o, o + WIDTH_A)
    qb_ref[...] = (q * QK_SCALE).astype(BF16)
    o += WIDTH_A
    k = proj(o, o + WIDTH_A)
    k_ref[...] = k
    kb_ref[...] = k.astype(BF16)
    o += WIDTH_A
    v = proj(o, o + WIDTH_A)
    v_ref[...] = v
    vb_ref[...] = v.astype(BF16)
    o += WIDTH_A
    ug_ref[...] = jax.nn.gelu(proj(o, o + WIDTH_B)).astype(ug_ref.dtype)
    o += WIDTH_B
    vg = jax.nn.gelu(proj(o, o + WIDTH_B))
    o += WIDTH_B
    mu = jnp.mean(vg, axis=-1, keepdims=True)
    vc = vg - mu
    vln = vc * lax.rsqrt(jnp.mean(vc * vc, axis=-1, keepdims=True) + EPS) * lng_ref[...] + lnb_ref[...]
    vln_ref[...] = vln.astype(vln_ref.dtype)
    ga_ref[...] = jax.nn.sigmoid(proj(o, o + D_MODEL)).astype(BF16)
    o += D_MODEL
    gb_ref[...] = jax.nn.sigmoid(proj(o, o + D_MODEL)).astype(BF16)

    f = _dot(h, wf_ref[...]) + bf_ref[...]
    lf = jnp.minimum(f, 0.0) - jnp.log1p(jnp.exp(-jnp.abs(f)))
    c = _dot_split(cum_ref[...], lf) + carry_ref[...]
    carry_ref[...] = c[c.shape[0] - 1:, :]
    lf_ref[...] = lf[:, :N_HEADS]
    c_ref[...] = c[:, :N_HEADS]


def _inproj(x2d, g, w_main, w_f, b_f, ln_g, ln_b, cum, *, tiles_per_seq, vln_dtype):
    T = x2d.shape[0]
    tm = TOKEN_TILE
    nt = T // tm
    row = lambda n: pl.BlockSpec((tm, n), lambda i: (i, 0))
    out_shape = (
        jax.ShapeDtypeStruct((T, WIDTH_A), BF16),
        jax.ShapeDtypeStruct((T, WIDTH_A), BF16),
        jax.ShapeDtypeStruct((T, WIDTH_A), BF16),
        jax.ShapeDtypeStruct((T, WIDTH_A), F32),
        jax.ShapeDtypeStruct((T, WIDTH_A), F32),
        jax.ShapeDtypeStruct((T, N_HEADS), F32),
        jax.ShapeDtypeStruct((T, N_HEADS), F32),
        jax.ShapeDtypeStruct((T, WIDTH_B), BF16),
        jax.ShapeDtypeStruct((T, WIDTH_B), vln_dtype),
        jax.ShapeDtypeStruct((T, D_MODEL), BF16),
        jax.ShapeDtypeStruct((T, D_MODEL), BF16),
    )
    out_specs = (row(WIDTH_A), row(WIDTH_A), row(WIDTH_A), row(WIDTH_A), row(WIDTH_A),
                 row(N_HEADS), row(N_HEADS), row(WIDTH_B), row(WIDTH_B), row(D_MODEL), row(D_MODEL))
    return pl.pallas_call(
        functools.partial(_inproj_kernel, tiles_per_seq=tiles_per_seq),
        grid=(nt,),
        in_specs=[row(D_MODEL), _const_spec(g.shape), _const_spec(w_main.shape), _const_spec(w_f.shape),
                  _const_spec(b_f.shape), _const_spec(ln_g.shape), _const_spec(ln_b.shape), _const_spec(cum.shape)],
        out_specs=out_specs,
        out_shape=out_shape,
        scratch_shapes=[pltpu.VMEM((1, LANES), F32)],
        compiler_params=pltpu.CompilerParams(dimension_semantics=("arbitrary",), vmem_limit_bytes=VMEM_LIMIT),
        name="inproj",
    )(x2d, g, w_main, w_f, b_f, ln_g, ln_b, cum)


def _fox_prompt_kernel(qt_ref, k_ref, vt_ref, o_ref, *, blk, n_heads):
    i = pl.program_id(2)
    r_io = lax.broadcasted_iota(jnp.int32, (blk, blk), 0)
    c_io = lax.broadcasted_iota(jnp.int32, (blk, blk), 1)

    def update(h, j, state, masked):
        m, l, acc = state
        off = pl.multiple_of(j * blk, blk)
        s = _dot(k_ref[0, h, pl.ds(off, blk), :], qt_ref[0, h, 0])
        if masked:
            s = jnp.where(r_io <= c_io, s, -jnp.inf)
        m_new = jnp.maximum(m, jnp.max(s, axis=0, keepdims=True))
        alpha = jnp.exp2(m - m_new)
        p = jnp.exp2(s - m_new)
        l = alpha * l + jnp.sum(p, axis=0, keepdims=True)
        acc = alpha * acc + _dot(vt_ref[0, h, j], p.astype(BF16))
        return m_new, l, acc

    def step(j, states, masked):
        return tuple(update(h, j, states[h], masked) for h in range(n_heads))

    init = tuple((jnp.full((1, blk), -jnp.inf, F32), jnp.zeros((1, blk), F32), jnp.zeros((HEAD_DIM, blk), F32))
                 for _ in range(n_heads))
    states = lax.fori_loop(0, i, lambda j, c: step(j, c, False), init)
    states = step(i, states, True)
    for h, (m, l, acc) in enumerate(states):
        o_ref[0, h] = (acc / l).astype(o_ref.dtype)


def _fox_prompt(qb, kb, vb, c):
    B, S, _ = qb.shape
    blk = min(ATTN_BLOCK, S)
    nb = S // blk
    hs = ATTN_HEADS_PER_STEP
    pad = jnp.zeros((B, N_HEADS, S, LANES - HEAD_DIM - 3), BF16)
    heads = lambda a: a.reshape(B, S, N_HEADS, HEAD_DIM).transpose(0, 2, 1, 3)
    qaug = jnp.concatenate([heads(qb), jnp.ones((B, N_HEADS, S, 3), BF16), pad], axis=-1)
    qt = qaug.reshape(B, N_HEADS, nb, blk, LANES).transpose(0, 1, 2, 4, 3)
    cneg = (c * (-LOG2E)).transpose(0, 2, 1)
    kaug = jnp.concatenate([heads(kb), jnp.stack(_split3(cneg), axis=-1), pad], axis=-1)
    vt = vb.reshape(B, nb, blk, N_HEADS, HEAD_DIM).transpose(0, 3, 1, 4, 2)
    ot = pl.pallas_call(
        functools.partial(_fox_prompt_kernel, blk=blk, n_heads=hs),
        grid=(B, N_HEADS // hs, nb),
        in_specs=[pl.BlockSpec((1, hs, 1, LANES, blk), lambda b, h, i: (b, h, i, 0, 0)),
                  pl.BlockSpec((1, hs, S, LANES), lambda b, h, i: (b, h, 0, 0)),
                  pl.BlockSpec((1, hs, nb, HEAD_DIM, blk), lambda b, h, i: (b, h, 0, 0, 0))],
        out_specs=pl.BlockSpec((1, hs, HEAD_DIM, blk), lambda b, h, i: (b, h, 0, i)),
        out_shape=jax.ShapeDtypeStruct((B, N_HEADS, HEAD_DIM, S), BF16),
        compiler_params=pltpu.CompilerParams(dimension_semantics=("arbitrary",) * 3, vmem_limit_bytes=VMEM_LIMIT),
        name="fox_prompt",
    )(qt, kaug, vt)
    return ot.transpose(0, 3, 1, 2).reshape(B, S, WIDTH_A)


def _suffix_matrices():
    l = np.arange(LANES)
    col = np.arange(N_HEADS * PAGE)
    s_c, h_c = col // N_HEADS, col % N_HEADS
    same_head = (l[:, None] % N_HEADS) == h_c[None, :]
    per_row = PAGE // 8
    later = (l[:, None] // N_HEADS) > (s_c[None, :] % per_row)
    return np.concatenate([same_head & later, same_head], axis=1).astype(np.float32)


def _fox_sample_kernel(pt_ref, *refs, n_pages_step):
    G = n_pages_step
    k_refs = refs[0:G]
    v_refs = refs[G:2 * G]
    lf_refs = refs[2 * G:3 * G]
    q_ref, kn_ref, vn_ref, cn_ref, sm_ref, o_ref, m_ref, l_ref, acc_ref, carry_ref = refs[3 * G:]
    jj = pl.program_id(1)
    n_steps = pl.num_programs(1)
    R = q_ref.shape[1]
    T = R // N_HEADS
    HP = N_HEADS * PAGE

    q = q_ref[0]

    def own_head(ncols):
        row_head = lax.broadcasted_iota(jnp.int32, (R, ncols), 0) // T
        col_head = lax.broadcasted_iota(jnp.int32, (R, ncols), 1) % N_HEADS
        return row_head == col_head

    def accumulate(s, v_list, ncols):
        m = m_ref[...]
        m_new = jnp.maximum(m, jnp.max(s, axis=-1, keepdims=True))
        alpha = jnp.exp2(m - m_new)
        p = jnp.exp2(s - m_new)
        l_ref[...] = alpha * l_ref[...] + jnp.sum(p, axis=-1, keepdims=True)
        acc = alpha * acc_ref[...]
        for g, vg in enumerate(v_list):
            acc = acc + _dot(p[:, g * ncols:(g + 1) * ncols].astype(BF16), vg)
        acc_ref[...] = acc
        m_ref[...] = m_new

    @pl.when(jj == 0)
    def _():
        carry_ref[...] = jnp.zeros_like(carry_ref)
        m_ref[...] = jnp.full(m_ref.shape, -jnp.inf, F32)
        l_ref[...] = jnp.zeros_like(l_ref)
        acc_ref[...] = jnp.zeros_like(acc_ref)
        s = _dot_nt(q, kn_ref[0].astype(BF16)) - cn_ref[0] * LOG2E
        t_row = lax.broadcasted_iota(jnp.int32, (R, R), 0) % T
        s_col = lax.broadcasted_iota(jnp.int32, (R, R), 1) // N_HEADS
        s = jnp.where(jnp.logical_and(own_head(R), s_col <= t_row), s, -jnp.inf)
        accumulate(s, [vn_ref[0].astype(BF16)], R)

    x = jnp.concatenate([r[0] for r in lf_refs], axis=0) * LOG2E
    hi, mid, lo = _split3(x)
    sm = sm_ref[...]
    y = _dot(hi, sm) + _dot(mid, sm) + _dot(lo, sm)
    r_io = lax.broadcasted_iota(jnp.int32, (8, HP), 0)
    s_blk = (lax.broadcasted_iota(jnp.int32, (8, HP), 1) // N_HEADS) // (PAGE // 8)
    carry = carry_ref[...]
    biases = [None] * G
    for g in reversed(range(G)):
        y1 = y[g * 8:(g + 1) * 8, :HP]
        y2 = y[g * 8:(g + 1) * 8, HP:]
        within = jnp.where(r_io == s_blk, y1, 0.0) + jnp.where(r_io > s_blk, y2, 0.0)
        biases[g] = jnp.sum(within, axis=0, keepdims=True) + carry
        carry = carry + jnp.sum(y2, axis=0, keepdims=True)
    carry_ref[...] = carry

    keep = own_head(HP)
    s_list = []
    for g in range(G):
        kn = k_refs[g][0].reshape(HP, HEAD_DIM).astype(BF16)
        s_list.append(jnp.where(keep, _dot_nt(q, kn) + biases[g], -jnp.inf))
    s = jnp.concatenate(s_list, axis=1) if G > 1 else s_list[0]
    accumulate(s, [v_refs[g][0].reshape(HP, HEAD_DIM).astype(BF16) for g in range(G)], HP)

    @pl.when(jj == n_steps - 1)
    def _():
        o_ref[0] = (acc_ref[...] / l_ref[...]).astype(o_ref.dtype)


def _fox_sample(q_ht, k_new, v_new, c_new, cache_k, cache_v, cache_logf, page_table):
    DB, R, _ = q_ht.shape
    n_pages = page_table.shape[1]
    G = min(PAGES_PER_STEP, n_pages)
    n_steps = n_pages // G
    sm = jnp.asarray(_suffix_matrices(), BF16)

    def page_spec(shape, g):
        nd = len(shape)
        return pl.BlockSpec(shape, lambda b, jj, pt, _g=g: (pt[b, (n_steps - 1 - jj) * G + _g],) + (0,) * (nd - 1))

    per_b = lambda shape: pl.BlockSpec(shape, lambda b, jj, pt: (b, 0, 0))
    kv_block = (1, PAGE, N_HEADS, HEAD_DIM)
    in_specs = ([page_spec(kv_block, g) for g in range(G)]
                + [page_spec(kv_block, g) for g in range(G)]
                + [page_spec((1, 8, LANES), g) for g in range(G)]
                + [per_b((1, R, HEAD_DIM)), per_b((1, R, HEAD_DIM)), per_b((1, R, HEAD_DIM)), per_b((1, 1, R)),
                   pl.BlockSpec(sm.shape, lambda b, jj, pt: (0, 0))])
    grid_spec = pltpu.PrefetchScalarGridSpec(
        num_scalar_prefetch=1,
        grid=(DB, n_steps),
        in_specs=in_specs,
        out_specs=pl.BlockSpec((1, R, HEAD_DIM), lambda b, jj, pt: (b, 0, 0)),
        scratch_shapes=[pltpu.VMEM((R, 1), F32), pltpu.VMEM((R, 1), F32),
                        pltpu.VMEM((R, HEAD_DIM), F32), pltpu.VMEM((1, N_HEADS * PAGE), F32)],
    )
    return pl.pallas_call(
        functools.partial(_fox_sample_kernel, n_pages_step=G),
        grid_spec=grid_spec,
        out_shape=jax.ShapeDtypeStruct((DB, R, HEAD_DIM), BF16),
        compiler_params=pltpu.CompilerParams(dimension_semantics=("arbitrary", "arbitrary"),
                                             vmem_limit_bytes=VMEM_LIMIT),
        name="fox_sample",
    )(page_table, *([cache_k] * G), *([cache_v] * G), *([cache_logf] * G), q_ht, k_new, v_new, c_new, sm)


def _merge_kernel(x_ref, a_ref, ug_ref, vln_ref, ga_ref, gb_ref, wsp_ref, bsp_ref, wua_ref, wub_ref, wo_ref,
                  gm_ref, wr_ref, br_ref, tri_ref,
                  x1_ref, h2_ref, meta_ref, cnt_ref, carry_ref, *, chunk_rows):
    i = pl.program_id(0)
    tm = x_ref.shape[0]

    @pl.when(i == 0)
    def _():
        carry_ref[...] = jnp.zeros_like(carry_ref)

    col_group = lax.broadcasted_iota(jnp.int32, (chunk_rows, WIDTH_B), 1) // GROUP_DIM_B
    sv_chunks = []
    for c in range(tm // chunk_rows):
        vc = vln_ref[c * chunk_rows:(c + 1) * chunk_rows, :].astype(BF16)
        sv = bsp_ref[...]
        for g in range(N_GROUPS_B):
            sv = sv + jnp.where(col_group == g, _dot(wsp_ref[g], vc), 0.0)
        sv_chunks.append(sv)
    sv = jnp.concatenate(sv_chunks, axis=0) if len(sv_chunks) > 1 else sv_chunks[0]
    b_out = (ug_ref[...].astype(F32) * sv).astype(BF16)

    m = (ga_ref[...].astype(F32) * _dot(a_ref[...], wua_ref[...])
         + gb_ref[...].astype(F32) * _dot(b_out, wub_ref[...]))
    x1 = x_ref[...] + _dot(m.astype(BF16), wo_ref[...])
    x1_ref[...] = x1

    h2 = _rmsnorm(x1, gm_ref[...])
    h2_ref[...] = h2
    logits = jnp.dot(h2, wr_ref[...], preferred_element_type=F32, precision=lax.Precision.HIGHEST) + br_ref[...]
    lane = lax.broadcasted_iota(jnp.int32, logits.shape, 1)
    lane_f = lane.astype(F32)
    big = jnp.float32(LANES)

    def first_lane_where(cond):
        return jnp.min(jnp.where(cond, lane_f, big), axis=-1, keepdims=True)

    gmask = lane < N_EXPERT_GROUPS
    gl = jnp.where(gmask, logits, -jnp.inf)
    gmax = jnp.max(gl, axis=-1, keepdims=True)
    gsel = first_lane_where(gl == gmax)
    pg = 1.0 / jnp.sum(jnp.exp(gl - gmax), axis=-1, keepdims=True)

    lo = ROUTER_EXPERT_LANE0 + EXPERTS_PER_GROUP * gsel
    emask = jnp.logical_and(lane_f >= lo, lane_f < lo + EXPERTS_PER_GROUP)
    el = jnp.where(emask, logits, -jnp.inf)
    emax = jnp.max(el, axis=-1, keepdims=True)
    ex = jnp.exp(el - emax)
    eprob = ex / jnp.sum(ex, axis=-1, keepdims=True)
    ep1 = jnp.where(emask, eprob, -1.0)
    v1 = jnp.max(ep1, axis=-1, keepdims=True)
    i1 = first_lane_where(ep1 == v1)
    ep2 = jnp.where(lane_f == i1, -1.0, ep1)
    v2 = jnp.max(ep2, axis=-1, keepdims=True)
    i2 = first_lane_where(ep2 == v2)
    w1 = pg * (v1 / (v1 + v2))
    w2 = pg * (v2 / (v1 + v2))

    sel1 = lane_f == i1
    sel2 = lane_f == i2
    onehot = jnp.where(jnp.logical_or(sel1, sel2), 1.0, 0.0)
    before = _dot(tri_ref[...], onehot.astype(BF16)) + carry_ref[...]
    r1 = jnp.sum(jnp.where(sel1, before, 0.0), axis=-1, keepdims=True)
    r2 = jnp.sum(jnp.where(sel2, before, 0.0), axis=-1, keepdims=True)
    carry_ref[...] = carry_ref[...] + jnp.sum(onehot, axis=0, keepdims=True)
    cnt_ref[...] = carry_ref[...]

    meta = jnp.zeros(logits.shape, F32)
    for ln, val in ((META_E0, i1 - ROUTER_EXPERT_LANE0), (META_E1, i2 - ROUTER_EXPERT_LANE0),
                    (META_R0, r1), (META_R1, r2), (META_W0, w1), (META_W1, w2)):
        meta = jnp.where(lane == ln, val, meta)
    meta_ref[...] = meta


def _merge(x2d, a, ug, vln, ga, gb, wsp, bsp, wua, wub, wo, gm, wr, br, tri):
    T = x2d.shape[0]
    tm = TOKEN_TILE
    row = lambda n: pl.BlockSpec((tm, n), lambda i: (i, 0))
    consts = (wsp, bsp, wua, wub, wo, gm, wr, br, tri)
    return pl.pallas_call(
        functools.partial(_merge_kernel, chunk_rows=wsp.shape[1]),
        grid=(T // tm,),
        in_specs=[row(D_MODEL), row(WIDTH_A), row(WIDTH_B), row(WIDTH_B), row(D_MODEL), row(D_MODEL)]
                 + [_const_spec(c.shape) for c in consts],
        out_specs=(row(D_MODEL), row(D_MODEL), row(LANES), _const_spec((1, LANES))),
        out_shape=(jax.ShapeDtypeStruct((T, D_MODEL), F32), jax.ShapeDtypeStruct((T, D_MODEL), F32),
                   jax.ShapeDtypeStruct((T, LANES), F32), jax.ShapeDtypeStruct((1, LANES), F32)),
        scratch_shapes=[pltpu.VMEM((1, LANES), F32)],
        compiler_params=pltpu.CompilerParams(dimension_semantics=("arbitrary",), vmem_limit_bytes=VMEM_LIMIT),
        name="merge_route",
    )(x2d, a, ug, vln, ga, gb, *consts)


def _dispatch_kernel(dest_ref, h_ref, xb_in_ref, xb_ref, sem):
    del xb_in_ref
    tm = h_ref.shape[0]

    def row_copy(r, k):
        return pltpu.make_async_copy(h_ref.at[pl.ds(r, 1), :], xb_ref.at[pl.ds(dest_ref[0, 0, 2 * r + k], 1), :], sem)

    def issue(r, _):
        row_copy(r, 0).start()
        row_copy(r, 1).start()
        return 0

    def drain(r, _):
        row_copy(r, 0).wait()
        row_copy(r, 1).wait()
        return 0

    lax.fori_loop(0, tm, issue, 0)
    lax.fori_loop(0, tm, drain, 0)


def _dispatch(h2, dest, xb):
    T = h2.shape[0]
    tm = TOKEN_TILE
    nt = T // tm
    dest3 = dest.reshape(nt, 1, 2 * tm)
    return pl.pallas_call(
        _dispatch_kernel,
        grid=(nt,),
        in_specs=[pl.BlockSpec((1, 1, 2 * tm), lambda i: (i, 0, 0), memory_space=pltpu.SMEM),
                  pl.BlockSpec((tm, D_MODEL), lambda i: (i, 0)),
                  pl.BlockSpec(memory_space=pl.ANY)],
        out_specs=pl.BlockSpec(memory_space=pl.ANY),
        out_shape=jax.ShapeDtypeStruct(xb.shape, xb.dtype),
        scratch_shapes=[pltpu.SemaphoreType.DMA(())],
        input_output_aliases={2: 0},
        compiler_params=pltpu.CompilerParams(dimension_semantics=("arbitrary",), has_side_effects=True),
        name="dispatch",
    )(dest3, h2, xb)


def _experts_kernel(be_ref, nu_ref, xb_ref, wg_ref, wu_ref, wd_ref, yb_ref):
    b = pl.program_id(0)

    @pl.when(b < nu_ref[0])
    def _():
        x = xb_ref[...].astype(BF16)
        hmid = jax.nn.silu(_dot(x, wg_ref[0])) * _dot(x, wu_ref[0])
        yb_ref[...] = _dot(hmid.astype(BF16), wd_ref[0])

    @pl.when(b >= nu_ref[0])
    def _():
        yb_ref[...] = jnp.zeros_like(yb_ref)


def _experts(xb, block_e, n_used, wg, wu, wd):
    P = xb.shape[0]
    bm = EXPERT_BLOCK
    nb = P // bm
    blk = lambda b, be, nu: (jnp.minimum(b, nu[0] - 1), 0)
    wsel = lambda b, be, nu: (be[jnp.minimum(b, nu[0] - 1)], 0, 0)
    grid_spec = pltpu.PrefetchScalarGridSpec(
        num_scalar_prefetch=2,
        grid=(nb,),
        in_specs=[pl.BlockSpec((bm, D_MODEL), blk),
                  pl.BlockSpec((1, D_MODEL, D_FF), wsel),
                  pl.BlockSpec((1, D_MODEL, D_FF), wsel),
                  pl.BlockSpec((1, D_FF, D_MODEL), wsel)],
        out_specs=pl.BlockSpec((bm, D_MODEL), lambda b, be, nu: (b, 0)),
    )
    return pl.pallas_call(
        _experts_kernel,
        grid_spec=grid_spec,
        out_shape=jax.ShapeDtypeStruct((P, D_MODEL), F32),
        compiler_params=pltpu.CompilerParams(dimension_semantics=("arbitrary",), vmem_limit_bytes=VMEM_LIMIT),
        name="experts",
    )(block_e, n_used, xb, wg, wu, wd)


def _combine_kernel(dest_ref, x1_ref, meta_ref, p_ref, yb_ref, gp_ref, wpg_ref, bpg_ref, wpp_ref, gpp_ref, gf_ref,
                    y_ref, buf_ref, sem):
    tm = x1_ref.shape[0]

    def row_copy(r, k):
        return pltpu.make_async_copy(yb_ref.at[pl.ds(dest_ref[0, 0, 2 * r + k], 1), :],
                                     buf_ref.at[k, pl.ds(r, 1), :], sem)

    def issue(r, _):
        row_copy(r, 0).start()
        row_copy(r, 1).start()
        return 0

    def drain(r, _):
        row_copy(r, 0).wait()
        row_copy(r, 1).wait()
        return 0

    lax.fori_loop(0, tm, issue, 0)
    e = _rmsnorm(_dot(p_ref[...].astype(BF16), wpp_ref[...]), gpp_ref[...])
    lax.fori_loop(0, tm, drain, 0)

    meta = meta_ref[...]
    w0 = meta[:, META_W0:META_W0 + 1]
    w1 = meta[:, META_W1:META_W1 + 1]
    x2 = x1_ref[...] + (buf_ref[0] * w0 + buf_ref[1] * w1)
    gate = jax.nn.sigmoid(_dot(_rmsnorm(x2, gp_ref[...]).astype(BF16), wpg_ref[...]) + bpg_ref[...])
    x3 = x2 + gate * e
    y_ref[...] = _rmsnorm(x3, gf_ref[...])


def _combine(x1, meta, p2d, dest, yb, gp, wpg, bpg, wpp, gpp, gf):
    T = x1.shape[0]
    tm = TOKEN_TILE
    nt = T // tm
    dest3 = dest.reshape(nt, 1, 2 * tm)
    row = lambda n: pl.BlockSpec((tm, n), lambda i: (i, 0))
    consts = (gp, wpg, bpg, wpp, gpp, gf)
    return pl.pallas_call(
        _combine_kernel,
        grid=(nt,),
        in_specs=[pl.BlockSpec((1, 1, 2 * tm), lambda i: (i, 0, 0), memory_space=pltpu.SMEM),
                  row(D_MODEL), row(LANES), row(p2d.shape[1]), pl.BlockSpec(memory_space=pl.ANY)]
                 + [_const_spec(c.shape) for c in consts],
        out_specs=row(D_MODEL),
        out_shape=jax.ShapeDtypeStruct((T, D_MODEL), F32),
        scratch_shapes=[pltpu.VMEM((2, tm, D_MODEL), F32), pltpu.SemaphoreType.DMA(())],
        compiler_params=pltpu.CompilerParams(dimension_semantics=("arbitrary",), vmem_limit_bytes=VMEM_LIMIT),
        name="combine_ple",
    )(dest3, x1, meta, p2d, yb, *consts)


def _cum_matrix(tm, seq_len):
    r = np.arange(tm)
    m = r[:, None] >= r[None, :]
    if seq_len < tm:
        m &= (r[:, None] // seq_len) == (r[None, :] // seq_len)
    return m.astype(np.float32)


def _layer(xp, xs, pp, ps, cache_k, cache_v, cache_logf, page_table, norm_mix_g, w_in, b_f, ln_v_g, ln_v_b,
           w_spatial, b_spatial, w_up_a, w_up_b, w_o, norm_moe_g, w_rg, b_rg, w_re, b_re, w_eg, w_eu, w_ed,
           norm_ple_g, w_ple_gate, b_ple_gate, w_ple_proj, norm_ple_post_g, norm_final_g):
    B, S, _ = xp.shape
    DB, T, _ = xs.shape
    tm = TOKEN_TILE
    Tp, Ts = B * S, DB * T
    assert Tp % tm == 0 and Ts % tm == 0 and tm % T == 0 and S % tm == 0

    off_f = 3 * WIDTH_A
    w_main = jnp.concatenate([w_in[:, :off_f], w_in[:, off_f + N_HEADS:]], axis=1).astype(BF16)
    w_f = jnp.pad(w_in[:, off_f:off_f + N_HEADS], ((0, 0), (0, LANES - N_HEADS))).astype(BF16)
    b_f128 = jnp.pad(b_f, (0, LANES - N_HEADS)).reshape(1, LANES)
    row1 = lambda v: v.reshape(1, -1)
    w_router = jnp.pad(jnp.concatenate([w_rg, w_re], axis=1), ((0, 0), (0, LANES - N_EXPERT_GROUPS - N_EXPERTS)))
    b_router = jnp.pad(jnp.concatenate([b_rg, b_re]), (0, LANES - N_EXPERT_GROUPS - N_EXPERTS)).reshape(1, LANES)
    tri = jnp.asarray(np.tril(np.ones((tm, tm), np.float32), -1), BF16)

    def spatial_params(seq_len):
        L = min(seq_len, CHUNK)
        w = w_spatial[:, :L, :L] * jnp.tril(jnp.ones((L, L), w_spatial.dtype))
        bias = jnp.repeat(b_spatial[:, :L].T, GROUP_DIM_B, axis=1)
        if L < CHUNK:
            reps = tm // L
            eye = jnp.eye(reps, dtype=w.dtype)
            w = jnp.einsum('ab,gts->gatbs', eye, w).reshape(N_GROUPS_B, tm, tm)
            bias = jnp.tile(bias, (reps, 1))
        return w.astype(BF16), bias

    common = (row1(norm_mix_g), w_main, w_f, b_f128, row1(ln_v_g), row1(ln_v_b))
    merge_w = (w_up_a.astype(BF16), w_up_b.astype(BF16), w_o.astype(BF16), row1(norm_moe_g), w_router, b_router, tri)

    cum_p = jnp.asarray(_cum_matrix(tm, S), BF16)
    qb, kb, vb, k_p, v_p, lf_p, c_p, ug_p, vln_p, ga_p, gb_p = _inproj(
        xp.reshape(Tp, D_MODEL), *common, cum_p, tiles_per_seq=S // tm, vln_dtype=BF16)
    a_p = _fox_prompt(qb.reshape(B, S, WIDTH_A), kb.reshape(B, S, WIDTH_A), vb.reshape(B, S, WIDTH_A),
                      c_p.reshape(B, S, N_HEADS))
    wsp_p, bsp_p = spatial_params(S)
    x1_p, h2_p, meta_p, cnt_p = _merge(xp.reshape(Tp, D_MODEL), a_p.reshape(Tp, WIDTH_A), ug_p, vln_p, ga_p, gb_p,
                                       wsp_p, bsp_p, *merge_w)

    cum_s = jnp.asarray(_cum_matrix(tm, T), BF16)
    qs, _, _, k_s, v_s, lf_s, c_s, ug_s, vln_s, ga_s, gb_s = _inproj(
        xs.reshape(Ts, D_MODEL), *common, cum_s, tiles_per_seq=1, vln_dtype=F32)
    n_pool = cache_k.shape[0]
    q_ht = qs.reshape(DB, T, N_HEADS, HEAD_DIM).transpose(0, 2, 1, 3).reshape(DB, N_HEADS * T, HEAD_DIM)
    a_ht = _fox_sample(q_ht, k_s.reshape(DB, T * N_HEADS, HEAD_DIM), v_s.reshape(DB, T * N_HEADS, HEAD_DIM),
                       c_s.reshape(DB, 1, T * N_HEADS), cache_k, cache_v, cache_logf.reshape(n_pool, 8, LANES),
                       page_table)
    a_s = a_ht.reshape(DB, N_HEADS, T, HEAD_DIM).transpose(0, 2, 1, 3)
    wsp_s, bsp_s = spatial_params(T)
    x1_s, h2_s, meta_s, cnt_s = _merge(xs.reshape(Ts, D_MODEL), a_s.reshape(Ts, WIDTH_A), ug_s, vln_s, ga_s, gb_s,
                                       wsp_s, bsp_s, *merge_w)

    bm = EXPERT_BLOCK
    e0 = ROUTER_EXPERT_LANE0
    counts_p = cnt_p[0, e0:e0 + N_EXPERTS].astype(jnp.int32)
    counts_s = cnt_s[0, e0:e0 + N_EXPERTS].astype(jnp.int32)
    counts = counts_p + counts_s
    padded = (counts + bm - 1) // bm * bm
    pend = jnp.cumsum(padded)
    pstart = pend - padded
    n_assign = 2 * (Tp + Ts)
    nb = -(-(n_assign + N_EXPERTS * (bm - 1)) // bm)
    block_e = jnp.minimum(jnp.searchsorted(pend, jnp.arange(nb, dtype=jnp.int32) * bm, side='right'),
                          N_EXPERTS - 1).astype(jnp.int32)
    n_used = (pend[-1:] // bm).astype(jnp.int32)

    def slots(meta, base):
        e = meta[:, META_E0:META_E1 + 1].astype(jnp.int32)
        r = meta[:, META_R0:META_R1 + 1].astype(jnp.int32)
        return base[e] + r

    dest_p = slots(meta_p, pstart)
    dest_s = slots(meta_s, pstart + counts_p)

    xb = jnp.zeros((nb * bm, D_MODEL), F32)
    xb = _dispatch(h2_p, dest_p, xb)
    xb = _dispatch(h2_s, dest_s, xb)
    yb = _experts(xb, block_e, n_used, w_eg.astype(BF16), w_eu.astype(BF16), w_ed.astype(BF16))

    ple_w = (row1(norm_ple_g), w_ple_gate.astype(BF16), row1(b_ple_gate), w_ple_proj.astype(BF16),
             row1(norm_ple_post_g), row1(norm_final_g))
    y_p = _combine(x1_p, meta_p, pp.reshape(Tp, -1), dest_p, yb, *ple_w)
    y_s = _combine(x1_s, meta_s, ps.reshape(Ts, -1), dest_s, yb, *ple_w)

    return (y_p.reshape(B, S, D_MODEL), y_s.reshape(DB, T, D_MODEL),
            k_p.reshape(B, S, N_HEADS, HEAD_DIM), v_p.reshape(B, S, N_HEADS, HEAD_DIM), lf_p.reshape(B, S, N_HEADS),
            k_s.reshape(DB, T, N_HEADS, HEAD_DIM), v_s.reshape(DB, T, N_HEADS, HEAD_DIM), lf_s.reshape(DB, T, N_HEADS),
            vln_s.reshape(DB, T, WIDTH_B))


def kernel(x_prompt, x_sample, p_prompt, p_sample, cache_k, cache_v, cache_logf, page_table, norm_mix_g, w_in, b_f, ln_v_g, ln_v_b, w_spatial, b_spatial, w_up_a, w_up_b, w_o, norm_moe_g, w_router_group, b_router_group, w_router_expert, b_router_expert, w_exp_gate, w_exp_up, w_exp_down, norm_ple_g, w_ple_gate, b_ple_gate, w_ple_proj, norm_ple_post_g, norm_final_g):
    depth = w_in.shape[0]
    assert depth == 1, "the final norm is fused into the layer's last kernel"
    outs = _layer(x_prompt, x_sample, p_prompt[0], p_sample[0], cache_k[0], cache_v[0], cache_logf[0], page_table,
                  norm_mix_g[0], w_in[0], b_f[0], ln_v_g[0], ln_v_b[0], w_spatial[0], b_spatial[0], w_up_a[0],
                  w_up_b[0], w_o[0], norm_moe_g[0], w_router_group[0], b_router_group[0], w_router_expert[0],
                  b_router_expert[0], w_exp_gate[0], w_exp_up[0], w_exp_down[0], norm_ple_g[0], w_ple_gate[0],
                  b_ple_gate[0], w_ple_proj[0], norm_ple_post_g[0], norm_final_g)
    y_p, y_s = outs[0], outs[1]
    return (y_p, y_s) + tuple(o[None] for o in outs[2:])
```

```python
import functools

import numpy as np
import jax
import jax.numpy as jnp
from jax import lax
from jax.experimental import pallas as pl
from jax.experimental.pallas import tpu as pltpu

F32 = jnp.float32
BF16 = jnp.bfloat16

D_MODEL = 1024
N_HEADS = 8
HEAD_DIM = 64
WIDTH_A = N_HEADS * HEAD_DIM
WIDTH_B = 512
N_GROUPS_B = 8
GROUP_DIM_B = WIDTH_B // N_GROUPS_B
CHUNK = 128
N_EXPERT_GROUPS = 4
EXPERTS_PER_GROUP = 8
N_EXPERTS = N_EXPERT_GROUPS * EXPERTS_PER_GROUP
D_FF = 512
PAGE = 128
EPS = 1e-6
LOG2E = 1.4426950408889634
QK_SCALE = HEAD_DIM ** -0.5 * LOG2E

LANES = 128
TOKEN_TILE = 256
EXPERT_BLOCK = 256
ATTN_BLOCK = 512
ATTN_HEADS_PER_STEP = 2
PAGES_PER_STEP = 8
VMEM_LIMIT = 48 * 1024 * 1024

META_E0, META_E1, META_R0, META_R1, META_W0, META_W1 = 0, 1, 2, 3, 4, 5
ROUTER_EXPERT_LANE0 = N_EXPERT_GROUPS


def _const_spec(shape):
    nd = len(shape)
    return pl.BlockSpec(shape, lambda *args, _nd=nd: (0,) * _nd, pipeline_mode=pl.Buffered(1))


def _rmsnorm(x, g):
    return x * lax.rsqrt(jnp.mean(x * x, axis=-1, keepdims=True) + EPS) * g


def _split3(x):
    hi = x.astype(BF16)
    r = x - hi.astype(F32)
    mid = r.astype(BF16)
    lo = (r - mid.astype(F32)).astype(BF16)
    return hi, mid, lo


def _dot(a, b):
    return jnp.dot(a, b, preferred_element_type=F32)


def _dot_nt(a, b):
    return lax.dot_general(a, b, (((1,), (1,)), ((), ())), preferred_element_type=F32)


def _dot_split(a_bf16, x_f32):
    hi, mid, lo = _split3(x_f32)
    return _dot(a_bf16, hi) + _dot(a_bf16, mid) + _dot(a_bf16, lo)


def _inproj_kernel(x_ref, g_ref, w_ref, wf_ref, bf_ref, lng_ref, lnb_ref, cum_ref, wqt_ref, wka_ref, wvt_ref, plc_ref,
                   q_ref, qt_ref, ka_ref, vt_ref, k_ref, v_ref, lf_ref, c_ref, ug_ref, vln_ref, ga_ref, gb_ref,
                   carry_ref, *, tiles_per_seq):
    i = pl.program_id(0)

    @pl.when(i % tiles_per_seq == 0)
    def _():
        carry_ref[...] = jnp.zeros_like(carry_ref)

    h = _rmsnorm(x_ref[...], g_ref[...]).astype(BF16)

    def proj(a, b):
        return _dot(h, w_ref[:, a:b])

    o = 0
    q_ref[...] = (proj(o, o + WIDTH_A) * QK_SCALE).astype(BF16)
    o += WIDTH_A
    k_ref[...] = proj(o, o + WIDTH_A)
    o += WIDTH_A
    v_ref[...] = proj(o, o + WIDTH_A)
    o += WIDTH_A
    ug_ref[...] = jax.nn.gelu(proj(o, o + WIDTH_B)).astype(ug_ref.dtype)
    o += WIDTH_B
    vg = jax.nn.gelu(proj(o, o + WIDTH_B))
    o += WIDTH_B
    mu = jnp.mean(vg, axis=-1, keepdims=True)
    vc = vg - mu
    vln = vc * lax.rsqrt(jnp.mean(vc * vc, axis=-1, keepdims=True) + EPS) * lng_ref[...] + lnb_ref[...]
    vln_ref[...] = vln.astype(vln_ref.dtype)
    ga_ref[...] = jax.nn.sigmoid(proj(o, o + D_MODEL)).astype(BF16)
    o += D_MODEL
    gb_ref[...] = jax.nn.sigmoid(proj(o, o + D_MODEL)).astype(BF16)

    f = _dot(h, wf_ref[...]) + bf_ref[...]
    lf = jnp.minimum(f, 0.0) - jnp.log1p(jnp.exp(-jnp.abs(f)))
    c = _dot_split(cum_ref[...], lf) + carry_ref[...]
    carry_ref[...] = c[c.shape[0] - 1:, :]
    lf_ref[...] = lf[:, :N_HEADS]
    c_ref[...] = c[:, :N_HEADS]

    qt = _dot_nt(wqt_ref[...], h) * QK_SCALE
    in_group = lax.broadcasted_iota(jnp.int32, qt.shape, 0) % LANES
    ones_rows = jnp.logical_and(in_group >= HEAD_DIM, in_group < HEAD_DIM + 3)
    qt_ref[0, 0] = jnp.where(ones_rows, 1.0, qt).astype(BF16)
    ka = _dot(h, wka_ref[...])
    for part, cpart in enumerate(_split3(c * (-LOG2E))):
        ka = ka + _dot(cpart, plc_ref[part])
    ka_ref[...] = ka.astype(BF16)
    vt_ref[0, 0] = _dot_nt(wvt_ref[...], h).astype(BF16)


def _inproj(x2d, g, w_main, w_f, b_f, ln_g, ln_b, cum, wqt, wka, wvt, plc, *, seq_len, attn_blk, vln_dtype):
    T = x2d.shape[0]
    tm = TOKEN_TILE
    nt = T // tm
    tps = max(seq_len // tm, 1)
    nseq = nt // tps
    r = attn_blk // tm
    nb = tps // r
    row = lambda n: pl.BlockSpec((tm, n), lambda i: (i, 0))
    tr = lambda n: pl.BlockSpec((1, 1, n, tm), lambda i: (i // tps, (i % tps) // r, 0, (i % tps) % r))
    out_shape = (
        jax.ShapeDtypeStruct((T, WIDTH_A), BF16),
        jax.ShapeDtypeStruct((nseq, nb, N_HEADS * LANES, attn_blk), BF16),
        jax.ShapeDtypeStruct((T, N_HEADS * LANES), BF16),
        jax.ShapeDtypeStruct((nseq, nb, WIDTH_A, attn_blk), BF16),
        jax.ShapeDtypeStruct((T, WIDTH_A), F32),
        jax.ShapeDtypeStruct((T, WIDTH_A), F32),
        jax.ShapeDtypeStruct((T, N_HEADS), F32),
        jax.ShapeDtypeStruct((T, N_HEADS), F32),
        jax.ShapeDtypeStruct((T, WIDTH_B), BF16),
        jax.ShapeDtypeStruct((T, WIDTH_B), vln_dtype),
        jax.ShapeDtypeStruct((T, D_MODEL), BF16),
        jax.ShapeDtypeStruct((T, D_MODEL), BF16),
    )
    out_specs = (row(WIDTH_A), tr(N_HEADS * LANES), row(N_HEADS * LANES), tr(WIDTH_A), row(WIDTH_A), row(WIDTH_A),
                 row(N_HEADS), row(N_HEADS), row(WIDTH_B), row(WIDTH_B), row(D_MODEL), row(D_MODEL))
    consts = (g, w_main, w_f, b_f, ln_g, ln_b, cum, wqt, wka, wvt, plc)
    return pl.pallas_call(
        functools.partial(_inproj_kernel, tiles_per_seq=tps),
        grid=(nt,),
        in_specs=[row(D_MODEL)] + [_const_spec(c.shape) for c in consts],
        out_specs=out_specs,
        out_shape=out_shape,
        scratch_shapes=[pltpu.VMEM((1, LANES), F32)],
        compiler_params=pltpu.CompilerParams(dimension_semantics=("arbitrary",), vmem_limit_bytes=VMEM_LIMIT),
        name="inproj",
    )(x2d, *consts)


def _fox_prompt_kernel(qt_ref, k_ref, vt_ref, o_ref, *, blk, n_heads):
    i = pl.program_id(2)
    r_io = lax.broadcasted_iota(jnp.int32, (blk, blk), 0)
    c_io = lax.broadcasted_iota(jnp.int32, (blk, blk), 1)

    def update(h, j, state, masked):
        m, l, acc = state
        off = pl.multiple_of(j * blk, blk)
        s = _dot(k_ref[0, pl.ds(off, blk), h * LANES:(h + 1) * LANES], qt_ref[0, 0, h * LANES:(h + 1) * LANES, :])
        if masked:
            s = jnp.where(r_io <= c_io, s, -jnp.inf)
        m_new = jnp.maximum(m, jnp.max(s, axis=0, keepdims=True))
        alpha = jnp.exp2(m - m_new)
        p = jnp.exp2(s - m_new)
        l = alpha * l + jnp.sum(p, axis=0, keepdims=True)
        acc = alpha * acc + _dot(vt_ref[0, j, h * HEAD_DIM:(h + 1) * HEAD_DIM, :], p.astype(BF16))
        return m_new, l, acc

    def step(j, states, masked):
        return tuple(update(h, j, states[h], masked) for h in range(n_heads))

    init = tuple((jnp.full((1, blk), -jnp.inf, F32), jnp.zeros((1, blk), F32), jnp.zeros((HEAD_DIM, blk), F32))
                 for _ in range(n_heads))
    states = lax.fori_loop(0, i, lambda j, c: step(j, c, False), init)
    states = step(i, states, True)
    out_t = jnp.concatenate([acc / l for (m, l, acc) in states], axis=0)
    o_ref[0] = out_t.T.astype(o_ref.dtype)


def _fox_prompt(qt, kaug, vt):
    B, nb, _, blk = qt.shape
    S = nb * blk
    hs = ATTN_HEADS_PER_STEP
    return pl.pallas_call(
        functools.partial(_fox_prompt_kernel, blk=blk, n_heads=hs),
        grid=(B, N_HEADS // hs, nb),
        in_specs=[pl.BlockSpec((1, 1, hs * LANES, blk), lambda b, h, i: (b, i, h, 0)),
                  pl.BlockSpec((1, S, hs * LANES), lambda b, h, i: (b, 0, h)),
                  pl.BlockSpec((1, nb, hs * HEAD_DIM, blk), lambda b, h, i: (b, 0, h, 0))],
        out_specs=pl.BlockSpec((1, blk, hs * HEAD_DIM), lambda b, h, i: (b, i, h)),
        out_shape=jax.ShapeDtypeStruct((B, S, WIDTH_A), BF16),
        compiler_params=pltpu.CompilerParams(dimension_semantics=("arbitrary",) * 3, vmem_limit_bytes=VMEM_LIMIT),
        name="fox_prompt",
    )(qt, kaug, vt)


def _suffix_matrix():
    s = np.arange(PAGE)
    later = s[:, None] > s[None, :]
    return np.concatenate([later, np.ones((PAGE, PAGE), bool)], axis=1).astype(np.float32)


def _fox_sample_kernel(pt_ref, *refs, n_pages_step):
    G = n_pages_step
    k_refs = refs[0:G]
    v_refs = refs[G:2 * G]
    lf_refs = refs[2 * G:3 * G]
    q_ref, knt_ref, vnt_ref, cnt_ref, um_ref, o_ref, m_ref, l_ref, acc_ref, carry_ref = refs[3 * G:]
    jj = pl.program_id(1)
    n_steps = pl.num_programs(1)
    T = q_ref.shape[1]
    R = N_HEADS * T

    q = q_ref[0]
    lane_head = lax.broadcasted_iota(jnp.int32, q.shape, 1) // HEAD_DIM
    qbd = jnp.concatenate([jnp.where(lane_head == h, q, jnp.zeros_like(q)) for h in range(N_HEADS)], axis=0)

    def head_rows(by_head):
        return jnp.concatenate([jnp.broadcast_to(by_head[h:h + 1, :], (T, PAGE)) for h in range(N_HEADS)], axis=0)

    def accumulate(s_list, vt_list):
        s = jnp.concatenate(s_list, axis=1) if len(s_list) > 1 else s_list[0]
        m = m_ref[...]
        m_new = jnp.maximum(m, jnp.max(s, axis=-1, keepdims=True))
        alpha = jnp.exp2(m - m_new)
        p = jnp.exp2(s - m_new)
        l_ref[...] = alpha * l_ref[...] + jnp.sum(p, axis=-1, keepdims=True)
        acc = alpha * acc_ref[...]
        for g, vt in enumerate(vt_list):
            acc = acc + _dot_nt(p[:, g * PAGE:(g + 1) * PAGE].astype(BF16), vt)
        acc_ref[...] = acc
        m_ref[...] = m_new

    @pl.when(jj == 0)
    def _():
        carry_ref[...] = jnp.zeros_like(carry_ref)
        m_ref[...] = jnp.full(m_ref.shape, -jnp.inf, F32)
        l_ref[...] = jnp.zeros_like(l_ref)
        acc_ref[...] = jnp.zeros_like(acc_ref)
        s = _dot(qbd, knt_ref[0].astype(BF16)) - head_rows(cnt_ref[0] * LOG2E)
        t_row = lax.broadcasted_iota(jnp.int32, (R, PAGE), 0) % T
        s_col = lax.broadcasted_iota(jnp.int32, (R, PAGE), 1)
        s = jnp.where(s_col <= t_row, s, -jnp.inf)
        accumulate([s], [vnt_ref[0].astype(BF16)])

    x = jnp.concatenate([r[0] for r in lf_refs], axis=0) * LOG2E
    um = um_ref[...]
    y = functools.reduce(jnp.add, [_dot(part, um) for part in _split3(x)])
    carry = carry_ref[...]
    biases = [None] * G
    for g in reversed(range(G)):
        yg = y[g * N_HEADS:(g + 1) * N_HEADS]
        biases[g] = head_rows(yg[:, :PAGE] + carry)
        carry = carry + yg[:, PAGE:]
    carry_ref[...] = carry

    page2d = lambda ref: ref[0].reshape(WIDTH_A, PAGE).astype(BF16)
    s_list = [_dot(qbd, page2d(k_refs[g])) + biases[g] for g in range(G)]
    accumulate(s_list, [page2d(v_refs[g]) for g in range(G)])

    @pl.when(jj == n_steps - 1)
    def _():
        out = acc_ref[...] / l_ref[...]
        res = jnp.zeros((T, WIDTH_A), F32)
        for h in range(N_HEADS):
            res = res + jnp.where(lane_head == h, out[h * T:(h + 1) * T, :], 0.0)
        o_ref[0] = res.astype(o_ref.dtype)


def _fox_sample(qb, knt, vnt, cnt, cache_kt, cache_vt, cache_lft, page_table):
    DB, T, _ = qb.shape
    n_pages = page_table.shape[1]
    G = min(PAGES_PER_STEP, n_pages)
    n_steps = n_pages // G
    um = jnp.asarray(_suffix_matrix(), BF16)

    def page_spec(shape, g):
        nd = len(shape)
        return pl.BlockSpec(shape, lambda b, jj, pt, _g=g: (pt[b, (n_steps - 1 - jj) * G + _g],) + (0,) * (nd - 1))

    per_b = lambda shape: pl.BlockSpec(shape, lambda b, jj, pt: (b, 0, 0))
    kv_block = (1, N_HEADS, HEAD_DIM, PAGE)
    in_specs = ([page_spec(kv_block, g) for g in range(G)]
                + [page_spec(kv_block, g) for g in range(G)]
                + [page_spec((1, N_HEADS, PAGE), g) for g in range(G)]
                + [per_b((1, T, WIDTH_A)), per_b((1, WIDTH_A, PAGE)), per_b((1, WIDTH_A, PAGE)),
                   per_b((1, N_HEADS, PAGE)), pl.BlockSpec(um.shape, lambda b, jj, pt: (0, 0))])
    grid_spec = pltpu.PrefetchScalarGridSpec(
        num_scalar_prefetch=1,
        grid=(DB, n_steps),
        in_specs=in_specs,
        out_specs=pl.BlockSpec((1, T, WIDTH_A), lambda b, jj, pt: (b, 0, 0)),
        scratch_shapes=[pltpu.VMEM((N_HEADS * T, 1), F32), pltpu.VMEM((N_HEADS * T, 1), F32),
                        pltpu.VMEM((N_HEADS * T, WIDTH_A), F32), pltpu.VMEM((N_HEADS, PAGE), F32)],
    )
    return pl.pallas_call(
        functools.partial(_fox_sample_kernel, n_pages_step=G),
        grid_spec=grid_spec,
        out_shape=jax.ShapeDtypeStruct((DB, T, WIDTH_A), BF16),
        compiler_params=pltpu.CompilerParams(dimension_semantics=("arbitrary", "arbitrary"),
                                             vmem_limit_bytes=VMEM_LIMIT),
        name="fox_sample",
    )(page_table, *([cache_kt] * G), *([cache_vt] * G), *([cache_lft] * G), qb, knt, vnt, cnt, um)


def _merge_kernel(x_ref, a_ref, ug_ref, vln_ref, ga_ref, gb_ref, wsp_ref, bsp_ref, wua_ref, wub_ref, wo_ref,
                  gm_ref, wr_ref, br_ref, tri_ref,
                  x1_ref, h2_ref, meta_ref, cnt_ref, carry_ref, *, chunk_rows):
    i = pl.program_id(0)
    tm = x_ref.shape[0]

    @pl.when(i == 0)
    def _():
        carry_ref[...] = jnp.zeros_like(carry_ref)

    col_group = lax.broadcasted_iota(jnp.int32, (chunk_rows, WIDTH_B), 1) // GROUP_DIM_B
    sv_chunks = []
    for c in range(tm // chunk_rows):
        vc = vln_ref[c * chunk_rows:(c + 1) * chunk_rows, :].astype(BF16)
        sv = bsp_ref[...]
        for g in range(N_GROUPS_B):
            sv = sv + jnp.where(col_group == g, _dot(wsp_ref[g], vc), 0.0)
        sv_chunks.append(sv)
    sv = jnp.concatenate(sv_chunks, axis=0) if len(sv_chunks) > 1 else sv_chunks[0]
    b_out = (ug_ref[...].astype(F32) * sv).astype(BF16)

    m = (ga_ref[...].astype(F32) * _dot(a_ref[...], wua_ref[...])
         + gb_ref[...].astype(F32) * _dot(b_out, wub_ref[...]))
    x1 = x_ref[...] + _dot(m.astype(BF16), wo_ref[...])
    x1_ref[...] = x1

    h2 = _rmsnorm(x1, gm_ref[...])
    h2_ref[...] = h2
    logits = jnp.dot(h2, wr_ref[...], preferred_element_type=F32, precision=lax.Precision.HIGHEST) + br_ref[...]
    lane = lax.broadcasted_iota(jnp.int32, logits.shape, 1)
    lane_f = lane.astype(F32)
    big = jnp.float32(LANES)

    def first_lane_where(cond):
        return jnp.min(jnp.where(cond, lane_f, big), axis=-1, keepdims=True)

    gmask = lane < N_EXPERT_GROUPS
    gl = jnp.where(gmask, logits, -jnp.inf)
    gmax = jnp.max(gl, axis=-1, keepdims=True)
    gsel = first_lane_where(gl == gmax)
    pg = 1.0 / jnp.sum(jnp.exp(gl - gmax), axis=-1, keepdims=True)

    lo = ROUTER_EXPERT_LANE0 + EXPERTS_PER_GROUP * gsel
    emask = jnp.logical_and(lane_f >= lo, lane_f < lo + EXPERTS_PER_GROUP)
    el = jnp.where(emask, logits, -jnp.inf)
    emax = jnp.max(el, axis=-1, keepdims=True)
    ex = jnp.exp(el - emax)
    eprob = ex / jnp.sum(ex, axis=-1, keepdims=True)
    ep1 = jnp.where(emask, eprob, -1.0)
    v1 = jnp.max(ep1, axis=-1, keepdims=True)
    i1 = first_lane_where(ep1 == v1)
    ep2 = jnp.where(lane_f == i1, -1.0, ep1)
    v2 = jnp.max(ep2, axis=-1, keepdims=True)
    i2 = first_lane_where(ep2 == v2)
    w1 = pg * (v1 / (v1 + v2))
    w2 = pg * (v2 / (v1 + v2))

    sel1 = lane_f == i1
    sel2 = lane_f == i2
    onehot = jnp.where(jnp.logical_or(sel1, sel2), 1.0, 0.0)
    before = _dot(tri_ref[...], onehot.astype(BF16)) + carry_ref[...]
    r1 = jnp.sum(jnp.where(sel1, before, 0.0), axis=-1, keepdims=True)
    r2 = jnp.sum(jnp.where(sel2, before, 0.0), axis=-1, keepdims=True)
    carry_ref[...] = carry_ref[...] + jnp.sum(onehot, axis=0, keepdims=True)
    cnt_ref[...] = carry_ref[...]

    meta = jnp.zeros(logits.shape, F32)
    for ln, val in ((META_E0, i1 - ROUTER_EXPERT_LANE0), (META_E1, i2 - ROUTER_EXPERT_LANE0),
                    (META_R0, r1), (META_R1, r2), (META_W0, w1), (META_W1, w2)):
        meta = jnp.where(lane == ln, val, meta)
    meta_ref[...] = meta


def _merge(x2d, a, ug, vln, ga, gb, wsp, bsp, wua, wub, wo, gm, wr, br, tri):
    T = x2d.shape[0]
    tm = TOKEN_TILE
    row = lambda n: pl.BlockSpec((tm, n), lambda i: (i, 0))
    consts = (wsp, bsp, wua, wub, wo, gm, wr, br, tri)
    return pl.pallas_call(
        functools.partial(_merge_kernel, chunk_rows=wsp.shape[1]),
        grid=(T // tm,),
        in_specs=[row(D_MODEL), row(WIDTH_A), row(WIDTH_B), row(WIDTH_B), row(D_MODEL), row(D_MODEL)]
                 + [_const_spec(c.shape) for c in consts],
        out_specs=(row(D_MODEL), row(D_MODEL), row(LANES), pl.BlockSpec((1, LANES), lambda i: (0, 0))),
        out_shape=(jax.ShapeDtypeStruct((T, D_MODEL), F32), jax.ShapeDtypeStruct((T, D_MODEL), F32),
                   jax.ShapeDtypeStruct((T, LANES), F32), jax.ShapeDtypeStruct((1, LANES), F32)),
        scratch_shapes=[pltpu.VMEM((1, LANES), F32)],
        compiler_params=pltpu.CompilerParams(dimension_semantics=("arbitrary",), vmem_limit_bytes=VMEM_LIMIT),
        name="merge_route",
    )(x2d, a, ug, vln, ga, gb, *consts)


def _dispatch_kernel(dest_ref, h_ref, xb_in_ref, xb_ref, sem):
    del xb_in_ref
    tm = h_ref.shape[0]

    def row_copy(r, k):
        return pltpu.make_async_copy(h_ref.at[pl.ds(r, 1), :], xb_ref.at[pl.ds(dest_ref[0, 0, 2 * r + k], 1), :], sem)

    def issue(r, _):
        row_copy(r, 0).start()
        row_copy(r, 1).start()
        return 0

    def drain(r, _):
        row_copy(r, 0).wait()
        row_copy(r, 1).wait()
        return 0

    lax.fori_loop(0, tm, issue, 0)
    lax.fori_loop(0, tm, drain, 0)


def _dispatch(h2, dest, xb):
    T = h2.shape[0]
    tm = TOKEN_TILE
    nt = T // tm
    dest3 = dest.reshape(nt, 1, 2 * tm)
    return pl.pallas_call(
        _dispatch_kernel,
        grid=(nt,),
        in_specs=[pl.BlockSpec((1, 1, 2 * tm), lambda i: (i, 0, 0), memory_space=pltpu.SMEM),
                  pl.BlockSpec((tm, D_MODEL), lambda i: (i, 0)),
                  pl.BlockSpec(memory_space=pl.ANY)],
        out_specs=pl.BlockSpec(memory_space=pl.ANY),
        out_shape=jax.ShapeDtypeStruct(xb.shape, xb.dtype),
        scratch_shapes=[pltpu.SemaphoreType.DMA(())],
        input_output_aliases={2: 0},
        compiler_params=pltpu.CompilerParams(dimension_semantics=("arbitrary",), has_side_effects=True),
        name="dispatch",
    )(dest3, h2, xb)


def _experts_kernel(be_ref, nu_ref, xb_ref, wg_ref, wu_ref, wd_ref, yb_ref):
    b = pl.program_id(0)

    @pl.when(b < nu_ref[0])
    def _():
        x = xb_ref[...].astype(BF16)
        hmid = jax.nn.silu(_dot(x, wg_ref[0])) * _dot(x, wu_ref[0])
        yb_ref[...] = _dot(hmid.astype(BF16), wd_ref[0])

    @pl.when(b >= nu_ref[0])
    def _():
        yb_ref[...] = jnp.zeros_like(yb_ref)


def _experts(xb, block_e, n_used, wg, wu, wd):
    P = xb.shape[0]
    bm = EXPERT_BLOCK
    nb = P // bm
    blk = lambda b, be, nu: (jnp.minimum(b, nu[0] - 1), 0)
    wsel = lambda b, be, nu: (be[jnp.minimum(b, nu[0] - 1)], 0, 0)
    grid_spec = pltpu.PrefetchScalarGridSpec(
        num_scalar_prefetch=2,
        grid=(nb,),
        in_specs=[pl.BlockSpec((bm, D_MODEL), blk),
                  pl.BlockSpec((1, D_MODEL, D_FF), wsel),
                  pl.BlockSpec((1, D_MODEL, D_FF), wsel),
                  pl.BlockSpec((1, D_FF, D_MODEL), wsel)],
        out_specs=pl.BlockSpec((bm, D_MODEL), lambda b, be, nu: (b, 0)),
    )
    return pl.pallas_call(
        _experts_kernel,
        grid_spec=grid_spec,
        out_shape=jax.ShapeDtypeStruct((P, D_MODEL), F32),
        compiler_params=pltpu.CompilerParams(dimension_semantics=("arbitrary",), vmem_limit_bytes=VMEM_LIMIT),
        name="experts",
    )(block_e, n_used, xb, wg, wu, wd)


def _combine_kernel(dest_ref, x1_ref, meta_ref, p_ref, yb_ref, gp_ref, wpg_ref, bpg_ref, wpp_ref, gpp_ref, gf_ref,
                    y_ref, buf_ref, sem):
    tm = x1_ref.shape[0]

    def row_copy(r, k):
        return pltpu.make_async_copy(yb_ref.at[pl.ds(dest_ref[0, 0, 2 * r + k], 1), :],
                                     buf_ref.at[k, pl.ds(r, 1), :], sem)

    def issue(r, _):
        row_copy(r, 0).start()
        row_copy(r, 1).start()
        return 0

    def drain(r, _):
        row_copy(r, 0).wait()
        row_copy(r, 1).wait()
        return 0

    lax.fori_loop(0, tm, issue, 0)
    e = _rmsnorm(_dot(p_ref[...].astype(BF16), wpp_ref[...]), gpp_ref[...])
    lax.fori_loop(0, tm, drain, 0)

    meta = meta_ref[...]
    w0 = meta[:, META_W0:META_W0 + 1]
    w1 = meta[:, META_W1:META_W1 + 1]
    x2 = x1_ref[...] + (buf_ref[0] * w0 + buf_ref[1] * w1)
    gate = jax.nn.sigmoid(_dot(_rmsnorm(x2, gp_ref[...]).astype(BF16), wpg_ref[...]) + bpg_ref[...])
    x3 = x2 + gate * e
    y_ref[...] = _rmsnorm(x3, gf_ref[...])


def _combine(x1, meta, p2d, dest, yb, gp, wpg, bpg, wpp, gpp, gf):
    T = x1.shape[0]
    tm = TOKEN_TILE
    nt = T // tm
    dest3 = dest.reshape(nt, 1, 2 * tm)
    row = lambda n: pl.BlockSpec((tm, n), lambda i: (i, 0))
    consts = (gp, wpg, bpg, wpp, gpp, gf)
    return pl.pallas_call(
        _combine_kernel,
        grid=(nt,),
        in_specs=[pl.BlockSpec((1, 1, 2 * tm), lambda i: (i, 0, 0), memory_space=pltpu.SMEM),
                  row(D_MODEL), row(LANES), row(p2d.shape[1]), pl.BlockSpec(memory_space=pl.ANY)]
                 + [_const_spec(c.shape) for c in consts],
        out_specs=row(D_MODEL),
        out_shape=jax.ShapeDtypeStruct((T, D_MODEL), F32),
        scratch_shapes=[pltpu.VMEM((2, tm, D_MODEL), F32), pltpu.SemaphoreType.DMA(())],
        compiler_params=pltpu.CompilerParams(dimension_semantics=("arbitrary",), vmem_limit_bytes=VMEM_LIMIT),
        name="combine_ple",
    )(dest3, x1, meta, p2d, yb, *consts)


def _cum_matrix(tm, seq_len):
    r = np.arange(tm)
    m = r[:, None] >= r[None, :]
    if seq_len < tm:
        m &= (r[:, None] // seq_len) == (r[None, :] // seq_len)
    return m.astype(np.float32)


def _bias_placement():
    p = np.zeros((3, LANES, N_HEADS * LANES), np.float32)
    for i in range(3):
        for h in range(N_HEADS):
            p[i, h, h * LANES + HEAD_DIM + i] = 1.0
    return p


def _layer(xp, xs, pp, ps, cache_k, cache_v, cache_logf, page_table, norm_mix_g, w_in, b_f, ln_v_g, ln_v_b,
           w_spatial, b_spatial, w_up_a, w_up_b, w_o, norm_moe_g, w_rg, b_rg, w_re, b_re, w_eg, w_eu, w_ed,
           norm_ple_g, w_ple_gate, b_ple_gate, w_ple_proj, norm_ple_post_g, norm_final_g):
    B, S, _ = xp.shape
    DB, T, _ = xs.shape
    tm = TOKEN_TILE
    Tp, Ts = B * S, DB * T
    blk = min(ATTN_BLOCK, S)
    assert Tp % tm == 0 and Ts % tm == 0 and tm % T == 0 and S % blk == 0 and blk % tm == 0

    off_f = 3 * WIDTH_A
    w_main = jnp.concatenate([w_in[:, :off_f], w_in[:, off_f + N_HEADS:]], axis=1).astype(BF16)
    w_f = jnp.pad(w_in[:, off_f:off_f + N_HEADS], ((0, 0), (0, LANES - N_HEADS))).astype(BF16)
    b_f128 = jnp.pad(b_f, (0, LANES - N_HEADS)).reshape(1, LANES)

    def head_groups(w):
        w3 = w.reshape(D_MODEL, N_HEADS, HEAD_DIM)
        return jnp.pad(w3, ((0, 0), (0, 0), (0, LANES - HEAD_DIM))).reshape(D_MODEL, N_HEADS * LANES)

    wqt = head_groups(w_in[:, :WIDTH_A]).T.astype(BF16)
    wka = head_groups(w_in[:, WIDTH_A:2 * WIDTH_A]).astype(BF16)
    wvt = w_in[:, 2 * WIDTH_A:3 * WIDTH_A].T.astype(BF16)
    plc = jnp.asarray(_bias_placement(), BF16)
    row1 = lambda v: v.reshape(1, -1)
    w_router = jnp.pad(jnp.concatenate([w_rg, w_re], axis=1), ((0, 0), (0, LANES - N_EXPERT_GROUPS - N_EXPERTS)))
    b_router = jnp.pad(jnp.concatenate([b_rg, b_re]), (0, LANES - N_EXPERT_GROUPS - N_EXPERTS)).reshape(1, LANES)
    tri = jnp.asarray(np.tril(np.ones((tm, tm), np.float32), -1), BF16)

    def spatial_params(seq_len):
        L = min(seq_len, CHUNK)
        w = w_spatial[:, :L, :L] * jnp.tril(jnp.ones((L, L), w_spatial.dtype))
        bias = jnp.repeat(b_spatial[:, :L].T, GROUP_DIM_B, axis=1)
        if L < CHUNK:
            reps = tm // L
            eye = jnp.eye(reps, dtype=w.dtype)
            w = jnp.einsum('ab,gts->gatbs', eye, w).reshape(N_GROUPS_B, tm, tm)
            bias = jnp.tile(bias, (reps, 1))
        return w.astype(BF16), bias

    def inproj(x2d, seq_len, attn_blk, vln_dtype):
        cum = jnp.asarray(_cum_matrix(tm, seq_len), BF16)
        return _inproj(x2d, row1(norm_mix_g), w_main, w_f, b_f128, row1(ln_v_g), row1(ln_v_b), cum, wqt, wka, wvt, plc,
                       seq_len=seq_len, attn_blk=attn_blk, vln_dtype=vln_dtype)

    merge_w = (w_up_a.astype(BF16), w_up_b.astype(BF16), w_o.astype(BF16), row1(norm_moe_g), w_router, b_router, tri)

    _, qt, kaug, vt, k_p, v_p, lf_p, _, ug_p, vln_p, ga_p, gb_p = inproj(xp.reshape(Tp, D_MODEL), S, blk, BF16)
    a_p = _fox_prompt(qt, kaug.reshape(B, S, N_HEADS * LANES), vt)
    wsp_p, bsp_p = spatial_params(S)
    x1_p, h2_p, meta_p, cnt_p = _merge(xp.reshape(Tp, D_MODEL), a_p.reshape(Tp, WIDTH_A), ug_p, vln_p, ga_p, gb_p,
                                       wsp_p, bsp_p, *merge_w)

    q_s, _, _, _, k_s, v_s, lf_s, c_s, ug_s, vln_s, ga_s, gb_s = inproj(xs.reshape(Ts, D_MODEL), T, tm, F32)
    cache_kt = jnp.transpose(cache_k, (0, 2, 3, 1))
    cache_vt = jnp.transpose(cache_v, (0, 2, 3, 1))
    cache_lft = jnp.transpose(cache_logf, (0, 2, 1))
    lane_pad = lambda a: jnp.pad(a, ((0, 0), (0, 0), (0, PAGE - T)))
    new_t = lambda a, n: lane_pad(a.reshape(DB, T, n).transpose(0, 2, 1))
    a_s = _fox_sample(q_s.reshape(DB, T, WIDTH_A), new_t(k_s, WIDTH_A), new_t(v_s, WIDTH_A), new_t(c_s, N_HEADS),
                      cache_kt, cache_vt, cache_lft, page_table)
    wsp_s, bsp_s = spatial_params(T)
    x1_s, h2_s, meta_s, cnt_s = _merge(xs.reshape(Ts, D_MODEL), a_s.reshape(Ts, WIDTH_A), ug_s, vln_s, ga_s, gb_s,
                                       wsp_s, bsp_s, *merge_w)

    bm = EXPERT_BLOCK
    e0 = ROUTER_EXPERT_LANE0
    counts_p = cnt_p[0, e0:e0 + N_EXPERTS].astype(jnp.int32)
    counts_s = cnt_s[0, e0:e0 + N_EXPERTS].astype(jnp.int32)
    counts = counts_p + counts_s
    padded = (counts + bm - 1) // bm * bm
    pend = jnp.cumsum(padded)
    pstart = pend - padded
    n_assign = 2 * (Tp + Ts)
    nb = -(-(n_assign + N_EXPERTS * (bm - 1)) // bm)
    block_start = jnp.arange(nb, dtype=jnp.int32) * bm
    block_e = jnp.minimum(jnp.sum(block_start[:, None] >= pend[None, :], axis=1), N_EXPERTS - 1).astype(jnp.int32)
    n_used = (pend[-1:] // bm).astype(jnp.int32)

    def slots(meta, base):
        e = meta[:, META_E0:META_E1 + 1].astype(jnp.int32)
        r = meta[:, META_R0:META_R1 + 1].astype(jnp.int32)
        return base[e] + r

    dest_p = slots(meta_p, pstart)
    dest_s = slots(meta_s, pstart + counts_p)

    xb = jnp.zeros((nb * bm, D_MODEL), F32)
    xb = _dispatch(h2_p, dest_p, xb)
    xb = _dispatch(h2_s, dest_s, xb)
    yb = _experts(xb, block_e, n_used, w_eg.astype(BF16), w_eu.astype(BF16), w_ed.astype(BF16))

    ple_w = (row1(norm_ple_g), w_ple_gate.astype(BF16), row1(b_ple_gate), w_ple_proj.astype(BF16),
             row1(norm_ple_post_g), row1(norm_final_g))
    y_p = _combine(x1_p, meta_p, pp.reshape(Tp, -1), dest_p, yb, *ple_w)
    y_s = _combine(x1_s, meta_s, ps.reshape(Ts, -1), dest_s, yb, *ple_w)

    return (y_p.reshape(B, S, D_MODEL), y_s.reshape(DB, T, D_MODEL),
            k_p.reshape(B, S, N_HEADS, HEAD_DIM), v_p.reshape(B, S, N_HEADS, HEAD_DIM), lf_p.reshape(B, S, N_HEADS),
            k_s.reshape(DB, T, N_HEADS, HEAD_DIM), v_s.reshape(DB, T, N_HEADS, HEAD_DIM), lf_s.reshape(DB, T, N_HEADS),
            vln_s.reshape(DB, T, WIDTH_B))


def kernel(x_prompt, x_sample, p_prompt, p_sample, cache_k, cache_v, cache_logf, page_table, norm_mix_g, w_in, b_f, ln_v_g, ln_v_b, w_spatial, b_spatial, w_up_a, w_up_b, w_o, norm_moe_g, w_router_group, b_router_group, w_router_expert, b_router_expert, w_exp_gate, w_exp_up, w_exp_down, norm_ple_g, w_ple_gate, b_ple_gate, w_ple_proj, norm_ple_post_g, norm_final_g):
    depth = w_in.shape[0]
    assert depth == 1, "the final norm is fused into the layer's last kernel"
    outs = _layer(x_prompt, x_sample, p_prompt[0], p_sample[0], cache_k[0], cache_v[0], cache_logf[0], page_table,
                  norm_mix_g[0], w_in[0], b_f[0], ln_v_g[0], ln_v_b[0], w_spatial[0], b_spatial[0], w_up_a[0],
                  w_up_b[0], w_o[0], norm_moe_g[0], w_router_group[0], b_router_group[0], w_router_expert[0],
                  b_router_expert[0], w_exp_gate[0], w_exp_up[0], w_exp_down[0], norm_ple_g[0], w_ple_gate[0],
                  b_ple_gate[0], w_ple_proj[0], norm_ple_post_g[0], norm_final_g)
    y_p, y_s = outs[0], outs[1]
    return (y_p, y_s) + tuple(o[None] for o in outs[2:])
```

```python
import functools

import numpy as np
import jax
import jax.numpy as jnp
from jax import lax
from jax.experimental import pallas as pl
from jax.experimental.pallas import tpu as pltpu

F32 = jnp.float32
BF16 = jnp.bfloat16

D_MODEL = 1024
N_HEADS = 8
HEAD_DIM = 64
WIDTH_A = N_HEADS * HEAD_DIM
WIDTH_B = 512
N_GROUPS_B = 8
GROUP_DIM_B = WIDTH_B // N_GROUPS_B
CHUNK = 128
N_EXPERT_GROUPS = 4
EXPERTS_PER_GROUP = 8
N_EXPERTS = N_EXPERT_GROUPS * EXPERTS_PER_GROUP
D_FF = 512
PAGE = 128
EPS = 1e-6
LOG2E = 1.4426950408889634
QK_SCALE = HEAD_DIM ** -0.5 * LOG2E

LANES = 128
TOKEN_TILE = 256
EXPERT_BLOCK = 256
ATTN_BLOCK = 512
ATTN_HEADS_PER_STEP = 2
PAGES_PER_STEP = 16
DMA_UNROLL = 8
VMEM_LIMIT = 48 * 1024 * 1024

META_E0, META_E1, META_R0, META_R1, META_W0, META_W1 = 0, 1, 2, 3, 4, 5
ROUTER_EXPERT_LANE0 = N_EXPERT_GROUPS


def _const_spec(shape):
    nd = len(shape)
    return pl.BlockSpec(shape, lambda *args, _nd=nd: (0,) * _nd, pipeline_mode=pl.Buffered(1))


def _rmsnorm(x, g):
    return x * lax.rsqrt(jnp.mean(x * x, axis=-1, keepdims=True) + EPS) * g


def _split3(x):
    hi = x.astype(BF16)
    r = x - hi.astype(F32)
    mid = r.astype(BF16)
    lo = (r - mid.astype(F32)).astype(BF16)
    return hi, mid, lo


def _dot(a, b):
    return jnp.dot(a, b, preferred_element_type=F32)


def _dot_nt(a, b):
    return lax.dot_general(a, b, (((1,), (1,)), ((), ())), preferred_element_type=F32)


def _dot_split(a_bf16, x_f32):
    hi, mid, lo = _split3(x_f32)
    return _dot(a_bf16, hi) + _dot(a_bf16, mid) + _dot(a_bf16, lo)


def _inproj_kernel(x_ref, g_ref, w_ref, wf_ref, bf_ref, lng_ref, lnb_ref, cum_ref, wqt_ref, wka_ref, wkt_ref, wvt_ref,
                   plc_ref,
                   q_ref, qt_ref, ka_ref, vt_ref, ktf_ref, vtf_ref, lf_ref, c_ref, ug_ref, vln_ref, ga_ref, gb_ref,
                   carry_ref, *, tiles_per_seq):
    i = pl.program_id(0)

    @pl.when(i % tiles_per_seq == 0)
    def _():
        carry_ref[...] = jnp.zeros_like(carry_ref)

    h = _rmsnorm(x_ref[...], g_ref[...]).astype(BF16)

    def proj(a, b):
        return _dot(h, w_ref[:, a:b])

    o = 0
    q_ref[...] = (proj(o, o + WIDTH_A) * QK_SCALE).astype(BF16)
    o += WIDTH_A
    ug_ref[...] = jax.nn.gelu(proj(o, o + WIDTH_B)).astype(ug_ref.dtype)
    o += WIDTH_B
    vg = jax.nn.gelu(proj(o, o + WIDTH_B))
    o += WIDTH_B
    mu = jnp.mean(vg, axis=-1, keepdims=True)
    vc = vg - mu
    vln = vc * lax.rsqrt(jnp.mean(vc * vc, axis=-1, keepdims=True) + EPS) * lng_ref[...] + lnb_ref[...]
    vln_ref[...] = vln.astype(vln_ref.dtype)
    ga_ref[...] = jax.nn.sigmoid(proj(o, o + D_MODEL)).astype(BF16)
    o += D_MODEL
    gb_ref[...] = jax.nn.sigmoid(proj(o, o + D_MODEL)).astype(BF16)

    f = _dot(h, wf_ref[...]) + bf_ref[...]
    lf = jnp.minimum(f, 0.0) - jnp.log1p(jnp.exp(-jnp.abs(f)))
    c = _dot_split(cum_ref[...], lf) + carry_ref[...]
    carry_ref[...] = c[c.shape[0] - 1:, :]
    lf_ref[...] = lf[:, :N_HEADS]
    c_ref[...] = c[:, :N_HEADS]

    qt = _dot_nt(wqt_ref[...], h) * QK_SCALE
    in_group = lax.broadcasted_iota(jnp.int32, qt.shape, 0) % LANES
    ones_rows = jnp.logical_and(in_group >= HEAD_DIM, in_group < HEAD_DIM + 3)
    qt_ref[0, 0] = jnp.where(ones_rows, 1.0, qt).astype(BF16)
    ka = _dot(h, wka_ref[...])
    for part, cpart in enumerate(_split3(c * (-LOG2E))):
        ka = ka + _dot(cpart, plc_ref[part])
    ka_ref[...] = ka.astype(BF16)
    ktf_ref[0] = _dot_nt(wkt_ref[...], h)
    vtf = _dot_nt(wvt_ref[...], h)
    vtf_ref[0] = vtf
    vt_ref[0, 0] = vtf.astype(BF16)


def _inproj(x2d, g, w_main, w_f, b_f, ln_g, ln_b, cum, wqt, wka, wkt, wvt, plc, *, seq_len, attn_blk, vln_dtype):
    T = x2d.shape[0]
    tm = TOKEN_TILE
    nt = T // tm
    tps = max(seq_len // tm, 1)
    nseq = nt // tps
    r = attn_blk // tm
    nb = tps // r
    row = lambda n: pl.BlockSpec((tm, n), lambda i: (i, 0))
    tr = lambda n: pl.BlockSpec((1, 1, n, tm), lambda i: (i // tps, (i % tps) // r, 0, (i % tps) % r))
    trf = pl.BlockSpec((1, WIDTH_A, tm), lambda i: (i // tps, 0, i % tps))
    out_shape = (
        jax.ShapeDtypeStruct((T, WIDTH_A), BF16),
        jax.ShapeDtypeStruct((nseq, nb, N_HEADS * LANES, attn_blk), BF16),
        jax.ShapeDtypeStruct((T, N_HEADS * LANES), BF16),
        jax.ShapeDtypeStruct((nseq, nb, WIDTH_A, attn_blk), BF16),
        jax.ShapeDtypeStruct((nseq, WIDTH_A, tps * tm), F32),
        jax.ShapeDtypeStruct((nseq, WIDTH_A, tps * tm), F32),
        jax.ShapeDtypeStruct((T, N_HEADS), F32),
        jax.ShapeDtypeStruct((T, N_HEADS), F32),
        jax.ShapeDtypeStruct((T, WIDTH_B), BF16),
        jax.ShapeDtypeStruct((T, WIDTH_B), vln_dtype),
        jax.ShapeDtypeStruct((T, D_MODEL), BF16),
        jax.ShapeDtypeStruct((T, D_MODEL), BF16),
    )
    out_specs = (row(WIDTH_A), tr(N_HEADS * LANES), row(N_HEADS * LANES), tr(WIDTH_A), trf, trf,
                 row(N_HEADS), row(N_HEADS), row(WIDTH_B), row(WIDTH_B), row(D_MODEL), row(D_MODEL))
    consts = (g, w_main, w_f, b_f, ln_g, ln_b, cum, wqt, wka, wkt, wvt, plc)
    return pl.pallas_call(
        functools.partial(_inproj_kernel, tiles_per_seq=tps),
        grid=(nt,),
        in_specs=[row(D_MODEL)] + [_const_spec(c.shape) for c in consts],
        out_specs=out_specs,
        out_shape=out_shape,
        scratch_shapes=[pltpu.VMEM((1, LANES), F32)],
        compiler_params=pltpu.CompilerParams(dimension_semantics=("arbitrary",), vmem_limit_bytes=VMEM_LIMIT),
        name="inproj",
    )(x2d, *consts)


def _fox_prompt_kernel(qt_ref, k_ref, vt_ref, o_ref, *, blk, n_heads):
    i = pl.program_id(2)
    r_io = lax.broadcasted_iota(jnp.int32, (blk, blk), 0)
    c_io = lax.broadcasted_iota(jnp.int32, (blk, blk), 1)

    def update(h, j, state, masked):
        m, l, acc = state
        off = pl.multiple_of(j * blk, blk)
        s = _dot(k_ref[0, pl.ds(off, blk), h * LANES:(h + 1) * LANES], qt_ref[0, 0, h * LANES:(h + 1) * LANES, :])
        if masked:
            s = jnp.where(r_io <= c_io, s, -jnp.inf)
        m_new = jnp.maximum(m, jnp.max(s, axis=0, keepdims=True))
        alpha = jnp.exp2(m - m_new)
        p = jnp.exp2(s - m_new)
        l = alpha * l + jnp.sum(p, axis=0, keepdims=True)
        acc = alpha * acc + _dot(vt_ref[0, j, h * HEAD_DIM:(h + 1) * HEAD_DIM, :], p.astype(BF16))
        return m_new, l, acc

    def step(j, states, masked):
        return tuple(update(h, j, states[h], masked) for h in range(n_heads))

    init = tuple((jnp.full((1, blk), -jnp.inf, F32), jnp.zeros((1, blk), F32), jnp.zeros((HEAD_DIM, blk), F32))
                 for _ in range(n_heads))
    states = lax.fori_loop(0, i, lambda j, c: step(j, c, False), init)
    states = step(i, states, True)
    out_t = jnp.concatenate([acc / l for (m, l, acc) in states], axis=0)
    o_ref[0] = out_t.T.astype(o_ref.dtype)


def _fox_prompt(qt, kaug, vt):
    B, nb, _, blk = qt.shape
    S = nb * blk
    hs = ATTN_HEADS_PER_STEP
    return pl.pallas_call(
        functools.partial(_fox_prompt_kernel, blk=blk, n_heads=hs),
        grid=(B, N_HEADS // hs, nb),
        in_specs=[pl.BlockSpec((1, 1, hs * LANES, blk), lambda b, h, i: (b, i, h, 0)),
                  pl.BlockSpec((1, S, hs * LANES), lambda b, h, i: (b, 0, h)),
                  pl.BlockSpec((1, nb, hs * HEAD_DIM, blk), lambda b, h, i: (b, 0, h, 0))],
        out_specs=pl.BlockSpec((1, blk, hs * HEAD_DIM), lambda b, h, i: (b, i, h)),
        out_shape=jax.ShapeDtypeStruct((B, S, WIDTH_A), BF16),
        compiler_params=pltpu.CompilerParams(dimension_semantics=("arbitrary",) * 3, vmem_limit_bytes=VMEM_LIMIT),
        name="fox_prompt",
    )(qt, kaug, vt)


def _suffix_matrix():
    s = np.arange(PAGE)
    later = s[:, None] > s[None, :]
    return np.concatenate([later, np.ones((PAGE, PAGE), bool)], axis=1).astype(np.float32)


def _fox_sample_kernel(pt_ref, *refs, n_pages_step):
    G = n_pages_step
    k_refs = refs[0:G]
    v_refs = refs[G:2 * G]
    lf_refs = refs[2 * G:3 * G]
    q_ref, knt_ref, vnt_ref, cnt_ref, um_ref, o_ref, m_ref, l_ref, acc_ref, carry_ref = refs[3 * G:]
    jj = pl.program_id(1)
    n_steps = pl.num_programs(1)
    T = q_ref.shape[1]
    R = N_HEADS * T

    q = q_ref[0]
    lane_head = lax.broadcasted_iota(jnp.int32, q.shape, 1) // HEAD_DIM
    qbd = jnp.concatenate([jnp.where(lane_head == h, q, jnp.zeros_like(q)) for h in range(N_HEADS)], axis=0)

    def head_rows(by_head):
        return jnp.concatenate([jnp.broadcast_to(by_head[h:h + 1, :], (T, PAGE)) for h in range(N_HEADS)], axis=0)

    def accumulate(s_list, vt_list):
        s = jnp.concatenate(s_list, axis=1) if len(s_list) > 1 else s_list[0]
        m = m_ref[...]
        m_new = jnp.maximum(m, jnp.max(s, axis=-1, keepdims=True))
        alpha = jnp.exp2(m - m_new)
        p = jnp.exp2(s - m_new)
        l_ref[...] = alpha * l_ref[...] + jnp.sum(p, axis=-1, keepdims=True)
        acc = alpha * acc_ref[...]
        for g, vt in enumerate(vt_list):
            acc = acc + _dot_nt(p[:, g * PAGE:(g + 1) * PAGE].astype(BF16), vt)
        acc_ref[...] = acc
        m_ref[...] = m_new

    @pl.when(jj == 0)
    def _():
        carry_ref[...] = jnp.zeros_like(carry_ref)
        m_ref[...] = jnp.full(m_ref.shape, -jnp.inf, F32)
        l_ref[...] = jnp.zeros_like(l_ref)
        acc_ref[...] = jnp.zeros_like(acc_ref)
        s = _dot(qbd, knt_ref[0].astype(BF16)) - head_rows(cnt_ref[0] * LOG2E)
        t_row = lax.broadcasted_iota(jnp.int32, (R, PAGE), 0) % T
        s_col = lax.broadcasted_iota(jnp.int32, (R, PAGE), 1)
        s = jnp.where(s_col <= t_row, s, -jnp.inf)
        accumulate([s], [vnt_ref[0].astype(BF16)])

    x = jnp.concatenate([r[0] for r in lf_refs], axis=0) * LOG2E
    um = um_ref[...]
    y = functools.reduce(jnp.add, [_dot(part, um) for part in _split3(x)])
    carry = carry_ref[...]
    biases = [None] * G
    for g in reversed(range(G)):
        yg = y[g * N_HEADS:(g + 1) * N_HEADS]
        biases[g] = head_rows(yg[:, :PAGE] + carry)
        carry = carry + yg[:, PAGE:]
    carry_ref[...] = carry

    page2d = lambda ref: ref[0].reshape(WIDTH_A, PAGE).astype(BF16)
    s_list = [_dot(qbd, page2d(k_refs[g])) + biases[g] for g in range(G)]
    accumulate(s_list, [page2d(v_refs[g]) for g in range(G)])

    @pl.when(jj == n_steps - 1)
    def _():
        out = acc_ref[...] / l_ref[...]
        res = jnp.zeros((T, WIDTH_A), F32)
        for h in range(N_HEADS):
            res = res + jnp.where(lane_head == h, out[h * T:(h + 1) * T, :], 0.0)
        o_ref[0] = res.astype(o_ref.dtype)


def _fox_sample(qb, knt, vnt, cnt, cache_kt, cache_vt, cache_lft, page_table):
    DB, T, _ = qb.shape
    n_pages = page_table.shape[1]
    G = min(PAGES_PER_STEP, n_pages)
    n_steps = n_pages // G
    um = jnp.asarray(_suffix_matrix(), BF16)

    def page_spec(shape, g):
        nd = len(shape)
        return pl.BlockSpec(shape, lambda b, jj, pt, _g=g: (pt[b, (n_steps - 1 - jj) * G + _g],) + (0,) * (nd - 1))

    per_b = lambda shape: pl.BlockSpec(shape, lambda b, jj, pt: (b, 0, 0))
    kv_block = (1, N_HEADS, HEAD_DIM, PAGE)
    in_specs = ([page_spec(kv_block, g) for g in range(G)]
                + [page_spec(kv_block, g) for g in range(G)]
                + [page_spec((1, N_HEADS, PAGE), g) for g in range(G)]
                + [per_b((1, T, WIDTH_A)), per_b((1, WIDTH_A, PAGE)), per_b((1, WIDTH_A, PAGE)),
                   per_b((1, N_HEADS, PAGE)), pl.BlockSpec(um.shape, lambda b, jj, pt: (0, 0))])
    grid_spec = pltpu.PrefetchScalarGridSpec(
        num_scalar_prefetch=1,
        grid=(DB, n_steps),
        in_specs=in_specs,
        out_specs=pl.BlockSpec((1, T, WIDTH_A), lambda b, jj, pt: (b, 0, 0)),
        scratch_shapes=[pltpu.VMEM((N_HEADS * T, 1), F32), pltpu.VMEM((N_HEADS * T, 1), F32),
                        pltpu.VMEM((N_HEADS * T, WIDTH_A), F32), pltpu.VMEM((N_HEADS, PAGE), F32)],
    )
    return pl.pallas_call(
        functools.partial(_fox_sample_kernel, n_pages_step=G),
        grid_spec=grid_spec,
        out_shape=jax.ShapeDtypeStruct((DB, T, WIDTH_A), BF16),
        compiler_params=pltpu.CompilerParams(dimension_semantics=("arbitrary", "arbitrary"),
                                             vmem_limit_bytes=VMEM_LIMIT),
        name="fox_sample",
    )(page_table, *([cache_kt] * G), *([cache_vt] * G), *([cache_lft] * G), qb, knt, vnt, cnt, um)


def _merge_kernel(x_ref, a_ref, ug_ref, vln_ref, ga_ref, gb_ref, wsp_ref, bsp_ref, wua_ref, wub_ref, wo_ref,
                  gm_ref, wr_ref, br_ref, tri_ref,
                  x1_ref, h2_ref, meta_ref, cnt_ref, carry_ref, *, chunk_rows):
    i = pl.program_id(0)
    tm = x_ref.shape[0]

    @pl.when(i == 0)
    def _():
        carry_ref[...] = jnp.zeros_like(carry_ref)

    col_group = lax.broadcasted_iota(jnp.int32, (chunk_rows, WIDTH_B), 1) // GROUP_DIM_B
    sv_chunks = []
    for c in range(tm // chunk_rows):
        vc = vln_ref[c * chunk_rows:(c + 1) * chunk_rows, :].astype(BF16)
        stacked = jnp.concatenate([jnp.where(col_group == g, vc, jnp.zeros_like(vc)) for g in range(N_GROUPS_B)],
                                  axis=0)
        sv_chunks.append(bsp_ref[...] + _dot(wsp_ref[...], stacked))
    sv = jnp.concatenate(sv_chunks, axis=0) if len(sv_chunks) > 1 else sv_chunks[0]
    b_out = (ug_ref[...].astype(F32) * sv).astype(BF16)

    m = (ga_ref[...].astype(F32) * _dot(a_ref[...], wua_ref[...])
         + gb_ref[...].astype(F32) * _dot(b_out, wub_ref[...]))
    x1 = x_ref[...] + _dot(m.astype(BF16), wo_ref[...])
    x1_ref[...] = x1

    h2 = _rmsnorm(x1, gm_ref[...])
    h2_ref[...] = h2
    logits = jnp.dot(h2, wr_ref[...], preferred_element_type=F32, precision=lax.Precision.HIGHEST) + br_ref[...]
    lane = lax.broadcasted_iota(jnp.int32, logits.shape, 1)
    lane_f = lane.astype(F32)
    big = jnp.float32(LANES)

    def first_lane_where(cond):
        return jnp.min(jnp.where(cond, lane_f, big), axis=-1, keepdims=True)

    gmask = lane < N_EXPERT_GROUPS
    gl = jnp.where(gmask, logits, -jnp.inf)
    gmax = jnp.max(gl, axis=-1, keepdims=True)
    gsel = first_lane_where(gl == gmax)
    pg = 1.0 / jnp.sum(jnp.exp(gl - gmax), axis=-1, keepdims=True)

    lo = ROUTER_EXPERT_LANE0 + EXPERTS_PER_GROUP * gsel
    emask = jnp.logical_and(lane_f >= lo, lane_f < lo + EXPERTS_PER_GROUP)
    el = jnp.where(emask, logits, -jnp.inf)
    emax = jnp.max(el, axis=-1, keepdims=True)
    ex = jnp.exp(el - emax)
    eprob = ex / jnp.sum(ex, axis=-1, keepdims=True)
    ep1 = jnp.where(emask, eprob, -1.0)
    v1 = jnp.max(ep1, axis=-1, keepdims=True)
    i1 = first_lane_where(ep1 == v1)
    ep2 = jnp.where(lane_f == i1, -1.0, ep1)
    v2 = jnp.max(ep2, axis=-1, keepdims=True)
    i2 = first_lane_where(ep2 == v2)
    w1 = pg * (v1 / (v1 + v2))
    w2 = pg * (v2 / (v1 + v2))

    sel1 = lane_f == i1
    sel2 = lane_f == i2
    onehot = jnp.where(jnp.logical_or(sel1, sel2), 1.0, 0.0)
    before = _dot(tri_ref[...], onehot.astype(BF16)) + carry_ref[...]
    r1 = jnp.sum(jnp.where(sel1, before, 0.0), axis=-1, keepdims=True)
    r2 = jnp.sum(jnp.where(sel2, before, 0.0), axis=-1, keepdims=True)
    carry_ref[...] = carry_ref[...] + jnp.sum(onehot, axis=0, keepdims=True)
    cnt_ref[...] = carry_ref[...]

    meta = jnp.zeros(logits.shape, F32)
    for ln, val in ((META_E0, i1 - ROUTER_EXPERT_LANE0), (META_E1, i2 - ROUTER_EXPERT_LANE0),
                    (META_R0, r1), (META_R1, r2), (META_W0, w1), (META_W1, w2)):
        meta = jnp.where(lane == ln, val, meta)
    meta_ref[...] = meta


def _merge(x2d, a, ug, vln, ga, gb, wsp, bsp, wua, wub, wo, gm, wr, br, tri):
    T = x2d.shape[0]
    tm = TOKEN_TILE
    row = lambda n: pl.BlockSpec((tm, n), lambda i: (i, 0))
    consts = (wsp, bsp, wua, wub, wo, gm, wr, br, tri)
    return pl.pallas_call(
        functools.partial(_merge_kernel, chunk_rows=wsp.shape[0]),
        grid=(T // tm,),
        in_specs=[row(D_MODEL), row(WIDTH_A), row(WIDTH_B), row(WIDTH_B), row(D_MODEL), row(D_MODEL)]
                 + [_const_spec(c.shape) for c in consts],
        out_specs=(row(D_MODEL), row(D_MODEL), row(LANES), pl.BlockSpec((1, LANES), lambda i: (0, 0))),
        out_shape=(jax.ShapeDtypeStruct((T, D_MODEL), F32), jax.ShapeDtypeStruct((T, D_MODEL), F32),
                   jax.ShapeDtypeStruct((T, LANES), F32), jax.ShapeDtypeStruct((1, LANES), F32)),
        scratch_shapes=[pltpu.VMEM((1, LANES), F32)],
        compiler_params=pltpu.CompilerParams(dimension_semantics=("arbitrary",), vmem_limit_bytes=VMEM_LIMIT),
        name="merge_route",
    )(x2d, a, ug, vln, ga, gb, *consts)


def _dispatch_kernel(dest_ref, h_ref, xb_in_ref, xb_ref, sem):
    del xb_in_ref
    tm = h_ref.shape[0]

    def row_copy(r, k):
        return pltpu.make_async_copy(h_ref.at[pl.ds(r, 1), :], xb_ref.at[pl.ds(dest_ref[0, 0, 2 * r + k], 1), :], sem)

    def issue(r, _):
        row_copy(r, 0).start()
        row_copy(r, 1).start()
        return 0

    def drain(r, _):
        row_copy(r, 0).wait()
        row_copy(r, 1).wait()
        return 0

    lax.fori_loop(0, tm, issue, 0, unroll=DMA_UNROLL)
    lax.fori_loop(0, tm, drain, 0, unroll=DMA_UNROLL)


def _dispatch(h2, dest, xb):
    T = h2.shape[0]
    tm = TOKEN_TILE
    nt = T // tm
    dest3 = dest.reshape(nt, 1, 2 * tm)
    return pl.pallas_call(
        _dispatch_kernel,
        grid=(nt,),
        in_specs=[pl.BlockSpec((1, 1, 2 * tm), lambda i: (i, 0, 0), memory_space=pltpu.SMEM),
                  pl.BlockSpec((tm, D_MODEL), lambda i: (i, 0)),
                  pl.BlockSpec(memory_space=pl.ANY)],
        out_specs=pl.BlockSpec(memory_space=pl.ANY),
        out_shape=jax.ShapeDtypeStruct(xb.shape, xb.dtype),
        scratch_shapes=[pltpu.SemaphoreType.DMA(())],
        input_output_aliases={2: 0},
        compiler_params=pltpu.CompilerParams(dimension_semantics=("arbitrary",), has_side_effects=True),
        name="dispatch",
    )(dest3, h2, xb)


def _experts_kernel(be_ref, nu_ref, xb_ref, wg_ref, wu_ref, wd_ref, yb_ref):
    b = pl.program_id(0)

    @pl.when(b < nu_ref[0])
    def _():
        x = xb_ref[...].astype(BF16)
        hmid = jax.nn.silu(_dot(x, wg_ref[0])) * _dot(x, wu_ref[0])
        yb_ref[...] = _dot(hmid.astype(BF16), wd_ref[0])

    @pl.when(b >= nu_ref[0])
    def _():
        yb_ref[...] = jnp.zeros_like(yb_ref)


def _experts(xb, block_e, n_used, wg, wu, wd):
    P = xb.shape[0]
    bm = EXPERT_BLOCK
    nb = P // bm
    blk = lambda b, be, nu: (jnp.minimum(b, nu[0] - 1), 0)
    wsel = lambda b, be, nu: (be[jnp.minimum(b, nu[0] - 1)], 0, 0)
    grid_spec = pltpu.PrefetchScalarGridSpec(
        num_scalar_prefetch=2,
        grid=(nb,),
        in_specs=[pl.BlockSpec((bm, D_MODEL), blk),
                  pl.BlockSpec((1, D_MODEL, D_FF), wsel),
                  pl.BlockSpec((1, D_MODEL, D_FF), wsel),
                  pl.BlockSpec((1, D_FF, D_MODEL), wsel)],
        out_specs=pl.BlockSpec((bm, D_MODEL), lambda b, be, nu: (b, 0)),
    )
    return pl.pallas_call(
        _experts_kernel,
        grid_spec=grid_spec,
        out_shape=jax.ShapeDtypeStruct((P, D_MODEL), F32),
        compiler_params=pltpu.CompilerParams(dimension_semantics=("arbitrary",), vmem_limit_bytes=VMEM_LIMIT),
        name="experts",
    )(block_e, n_used, xb, wg, wu, wd)


def _combine_kernel(dest_ref, dest_next_ref, x1_ref, meta_ref, p_ref, yb_ref, gp_ref, wpg_ref, bpg_ref, wpp_ref,
                    gpp_ref, gf_ref, y_ref, buf_ref, sems):
    i = pl.program_id(0)
    tm = x1_ref.shape[0]
    slot = i % 2

    def row_copy(d_ref, s, r, k):
        return pltpu.make_async_copy(yb_ref.at[pl.ds(d_ref[0, 0, 2 * r + k], 1), :],
                                     buf_ref.at[s, k, pl.ds(r, 1), :], sems.at[s])

    def issue(d_ref, s):
        def body(r, _):
            row_copy(d_ref, s, r, 0).start()
            row_copy(d_ref, s, r, 1).start()
            return 0
        lax.fori_loop(0, tm, body, 0, unroll=DMA_UNROLL)

    def drain(d_ref, s):
        def body(r, _):
            row_copy(d_ref, s, r, 0).wait()
            row_copy(d_ref, s, r, 1).wait()
            return 0
        lax.fori_loop(0, tm, body, 0, unroll=DMA_UNROLL)

    @pl.when(i == 0)
    def _():
        issue(dest_ref, 0)

    @pl.when(i + 1 < pl.num_programs(0))
    def _():
        issue(dest_next_ref, 1 - slot)

    e = _rmsnorm(_dot(p_ref[...].astype(BF16), wpp_ref[...]), gpp_ref[...])
    drain(dest_ref, slot)

    meta = meta_ref[...]
    w0 = meta[:, META_W0:META_W0 + 1]
    w1 = meta[:, META_W1:META_W1 + 1]
    x2 = x1_ref[...] + (buf_ref[slot, 0] * w0 + buf_ref[slot, 1] * w1)
    gate = jax.nn.sigmoid(_dot(_rmsnorm(x2, gp_ref[...]).astype(BF16), wpg_ref[...]) + bpg_ref[...])
    x3 = x2 + gate * e
    y_ref[...] = _rmsnorm(x3, gf_ref[...])


def _combine(x1, meta, p2d, dest, yb, gp, wpg, bpg, wpp, gpp, gf):
    T = x1.shape[0]
    tm = TOKEN_TILE
    nt = T // tm
    dest3 = dest.reshape(nt, 1, 2 * tm)
    row = lambda n: pl.BlockSpec((tm, n), lambda i: (i, 0))
    consts = (gp, wpg, bpg, wpp, gpp, gf)
    return pl.pallas_call(
        _combine_kernel,
        grid=(nt,),
        in_specs=[pl.BlockSpec((1, 1, 2 * tm), lambda i: (i, 0, 0), memory_space=pltpu.SMEM),
                  pl.BlockSpec((1, 1, 2 * tm), lambda i: (jnp.minimum(i + 1, nt - 1), 0, 0), memory_space=pltpu.SMEM),
                  row(D_MODEL), row(LANES), row(p2d.shape[1]), pl.BlockSpec(memory_space=pl.ANY)]
                 + [_const_spec(c.shape) for c in consts],
        out_specs=row(D_MODEL),
        out_shape=jax.ShapeDtypeStruct((T, D_MODEL), F32),
        scratch_shapes=[pltpu.VMEM((2, 2, tm, D_MODEL), F32), pltpu.SemaphoreType.DMA((2,))],
        compiler_params=pltpu.CompilerParams(dimension_semantics=("arbitrary",), vmem_limit_bytes=VMEM_LIMIT),
        name="combine_ple",
    )(dest3, dest3, x1, meta, p2d, yb, *consts)


def _cum_matrix(tm, seq_len):
    r = np.arange(tm)
    m = r[:, None] >= r[None, :]
    if seq_len < tm:
        m &= (r[:, None] // seq_len) == (r[None, :] // seq_len)
    return m.astype(np.float32)


def _bias_placement():
    p = np.zeros((3, LANES, N_HEADS * LANES), np.float32)
    for i in range(3):
        for h in range(N_HEADS):
            p[i, h, h * LANES + HEAD_DIM + i] = 1.0
    return p


def _layer(xp, xs, pp, ps, cache_k, cache_v, cache_logf, page_table, norm_mix_g, w_in, b_f, ln_v_g, ln_v_b,
           w_spatial, b_spatial, w_up_a, w_up_b, w_o, norm_moe_g, w_rg, b_rg, w_re, b_re, w_eg, w_eu, w_ed,
           norm_ple_g, w_ple_gate, b_ple_gate, w_ple_proj, norm_ple_post_g, norm_final_g):
    B, S, _ = xp.shape
    DB, T, _ = xs.shape
    tm = TOKEN_TILE
    Tp, Ts = B * S, DB * T
    blk = min(ATTN_BLOCK, S)
    assert Tp % tm == 0 and Ts % tm == 0 and tm % T == 0 and S % blk == 0 and blk % tm == 0

    off_f = 3 * WIDTH_A
    w_main = jnp.concatenate([w_in[:, :WIDTH_A], w_in[:, off_f + N_HEADS:]], axis=1).astype(BF16)
    w_f = jnp.pad(w_in[:, off_f:off_f + N_HEADS], ((0, 0), (0, LANES - N_HEADS))).astype(BF16)
    b_f128 = jnp.pad(b_f, (0, LANES - N_HEADS)).reshape(1, LANES)

    def head_groups(w):
        w3 = w.reshape(D_MODEL, N_HEADS, HEAD_DIM)
        return jnp.pad(w3, ((0, 0), (0, 0), (0, LANES - HEAD_DIM))).reshape(D_MODEL, N_HEADS * LANES)

    wqt = head_groups(w_in[:, :WIDTH_A]).T.astype(BF16)
    wka = head_groups(w_in[:, WIDTH_A:2 * WIDTH_A]).astype(BF16)
    wkt = w_in[:, WIDTH_A:2 * WIDTH_A].T.astype(BF16)
    wvt = w_in[:, 2 * WIDTH_A:3 * WIDTH_A].T.astype(BF16)
    plc = jnp.asarray(_bias_placement(), BF16)
    row1 = lambda v: v.reshape(1, -1)
    w_router = jnp.pad(jnp.concatenate([w_rg, w_re], axis=1), ((0, 0), (0, LANES - N_EXPERT_GROUPS - N_EXPERTS)))
    b_router = jnp.pad(jnp.concatenate([b_rg, b_re]), (0, LANES - N_EXPERT_GROUPS - N_EXPERTS)).reshape(1, LANES)
    tri = jnp.asarray(np.tril(np.ones((tm, tm), np.float32), -1), BF16)

    def spatial_params(seq_len):
        L = min(seq_len, CHUNK)
        w = w_spatial[:, :L, :L] * jnp.tril(jnp.ones((L, L), w_spatial.dtype))
        bias = jnp.repeat(b_spatial[:, :L].T, GROUP_DIM_B, axis=1)
        if L < CHUNK:
            reps = tm // L
            eye = jnp.eye(reps, dtype=w.dtype)
            w = jnp.einsum('ab,gts->gatbs', eye, w).reshape(N_GROUPS_B, tm, tm)
            bias = jnp.tile(bias, (reps, 1))
        rows = w.shape[1]
        return w.transpose(1, 0, 2).reshape(rows, N_GROUPS_B * rows).astype(BF16), bias

    def inproj(x2d, seq_len, attn_blk, vln_dtype):
        cum = jnp.asarray(_cum_matrix(tm, seq_len), BF16)
        return _inproj(x2d, row1(norm_mix_g), w_main, w_f, b_f128, row1(ln_v_g), row1(ln_v_b), cum, wqt, wka, wkt, wvt,
                       plc, seq_len=seq_len, attn_blk=attn_blk, vln_dtype=vln_dtype)

    merge_w = (w_up_a.astype(BF16), w_up_b.astype(BF16), w_o.astype(BF16), row1(norm_moe_g), w_router, b_router, tri)

    _, qt, kaug, vt, kt_p, vt_p, lf_p, _, ug_p, vln_p, ga_p, gb_p = inproj(xp.reshape(Tp, D_MODEL), S, blk, BF16)
    a_p = _fox_prompt(qt, kaug.reshape(B, S, N_HEADS * LANES), vt)
    wsp_p, bsp_p = spatial_params(S)
    x1_p, h2_p, meta_p, cnt_p = _merge(xp.reshape(Tp, D_MODEL), a_p.reshape(Tp, WIDTH_A), ug_p, vln_p, ga_p, gb_p,
                                       wsp_p, bsp_p, *merge_w)

    q_s, _, _, _, kt_s, vt_s, lf_s, c_s, ug_s, vln_s, ga_s, gb_s = inproj(xs.reshape(Ts, D_MODEL), T, tm, F32)
    per_seq = lambda a: a.reshape(WIDTH_A, DB, T).transpose(1, 0, 2)
    kt_s, vt_s = per_seq(kt_s), per_seq(vt_s)
    cache_kt = jnp.transpose(cache_k, (0, 2, 3, 1))
    cache_vt = jnp.transpose(cache_v, (0, 2, 3, 1))
    cache_lft = jnp.transpose(cache_logf, (0, 2, 1))
    lane_pad = lambda a: jnp.pad(a, ((0, 0), (0, 0), (0, PAGE - T)))
    cnt = lane_pad(c_s.reshape(DB, T, N_HEADS).transpose(0, 2, 1))
    a_s = _fox_sample(q_s.reshape(DB, T, WIDTH_A), lane_pad(kt_s), lane_pad(vt_s), cnt,
                      cache_kt, cache_vt, cache_lft, page_table)
    wsp_s, bsp_s = spatial_params(T)
    x1_s, h2_s, meta_s, cnt_s = _merge(xs.reshape(Ts, D_MODEL), a_s.reshape(Ts, WIDTH_A), ug_s, vln_s, ga_s, gb_s,
                                       wsp_s, bsp_s, *merge_w)

    bm = EXPERT_BLOCK
    e0 = ROUTER_EXPERT_LANE0
    counts_p = cnt_p[0, e0:e0 + N_EXPERTS].astype(jnp.int32)
    counts_s = cnt_s[0, e0:e0 + N_EXPERTS].astype(jnp.int32)
    counts = counts_p + counts_s
    padded = (counts + bm - 1) // bm * bm
    pend = jnp.cumsum(padded)
    pstart = pend - padded
    n_assign = 2 * (Tp + Ts)
    nb = -(-(n_assign + N_EXPERTS * (bm - 1)) // bm)
    block_start = jnp.arange(nb, dtype=jnp.int32) * bm
    block_e = jnp.minimum(jnp.sum(block_start[:, None] >= pend[None, :], axis=1), N_EXPERTS - 1).astype(jnp.int32)
    n_used = (pend[-1:] // bm).astype(jnp.int32)

    def slots(meta, base):
        e = meta[:, META_E0:META_E1 + 1].astype(jnp.int32)
        r = meta[:, META_R0:META_R1 + 1].astype(jnp.int32)
        return base[e] + r

    dest_p = slots(meta_p, pstart)
    dest_s = slots(meta_s, pstart + counts_p)

    xb = jnp.zeros((nb * bm, D_MODEL), F32)
    xb = _dispatch(h2_p, dest_p, xb)
    xb = _dispatch(h2_s, dest_s, xb)
    yb = _experts(xb, block_e, n_used, w_eg.astype(BF16), w_eu.astype(BF16), w_ed.astype(BF16))

    ple_w = (row1(norm_ple_g), w_ple_gate.astype(BF16), row1(b_ple_gate), w_ple_proj.astype(BF16),
             row1(norm_ple_post_g), row1(norm_final_g))
    y_p = _combine(x1_p, meta_p, pp.reshape(Tp, -1), dest_p, yb, *ple_w)
    y_s = _combine(x1_s, meta_s, ps.reshape(Ts, -1), dest_s, yb, *ple_w)

    token_major = lambda a: a.reshape(a.shape[0], N_HEADS, HEAD_DIM, a.shape[2]).transpose(0, 3, 1, 2)
    return (y_p.reshape(B, S, D_MODEL), y_s.reshape(DB, T, D_MODEL),
            token_major(kt_p), token_major(vt_p), lf_p.reshape(B, S, N_HEADS),
            token_major(kt_s), token_major(vt_s), lf_s.reshape(DB, T, N_HEADS),
            vln_s.reshape(DB, T, WIDTH_B))


def kernel(x_prompt, x_sample, p_prompt, p_sample, cache_k, cache_v, cache_logf, page_table, norm_mix_g, w_in, b_f, ln_v_g, ln_v_b, w_spatial, b_spatial, w_up_a, w_up_b, w_o, norm_moe_g, w_router_group, b_router_group, w_router_expert, b_router_expert, w_exp_gate, w_exp_up, w_exp_down, norm_ple_g, w_ple_gate, b_ple_gate, w_ple_proj, norm_ple_post_g, norm_final_g):
    depth = w_in.shape[0]
    assert depth == 1, "the final norm is fused into the layer's last kernel"
    outs = _layer(x_prompt, x_sample, p_prompt[0], p_sample[0], cache_k[0], cache_v[0], cache_logf[0], page_table,
                  norm_mix_g[0], w_in[0], b_f[0], ln_v_g[0], ln_v_b[0], w_spatial[0], b_spatial[0], w_up_a[0],
                  w_up_b[0], w_o[0], norm_moe_g[0], w_router_group[0], b_router_group[0], w_router_expert[0],
                  b_router_expert[0], w_exp_gate[0], w_exp_up[0], w_exp_down[0], norm_ple_g[0], w_ple_gate[0],
                  b_ple_gate[0], w_ple_proj[0], norm_ple_post_g[0], norm_final_g)
    y_p, y_s = outs[0], outs[1]
    return (y_p, y_s) + tuple(o[None] for o in outs[2:])
```
